```python
import math
import jax, jax.numpy as jnp
from jax import lax
import numpy as np

D_MODEL = 1024
BATCH = 8
SEQ = 4096
DEPTH = 4

f32 = jnp.float32
NORM_EPS = 1e-6
MIXER_CYCLE = 3

RET_HEADS = 8
RET_DK = D_MODEL // RET_HEADS
RET_DV = D_MODEL // RET_HEADS
RET_CHUNK = 128
RET_THETA = 10000.0

GDN_HEADS = 8
GDN_DK = D_MODEL // GDN_HEADS
GDN_DV = D_MODEL // GDN_HEADS
GDN_CONV = 4
GDN_CHUNK = 64

NSA_HEADS = 16
NSA_GROUPS = 4
NSA_HPG = NSA_HEADS // NSA_GROUPS
NSA_DH = D_MODEL // NSA_HEADS
NSA_CMP_BLOCK = 32
NSA_CMP_STRIDE = 16
NSA_CMP_HIDDEN = 256
NSA_SEL_BLOCK = 64
NSA_TOPN = 16
NSA_WINDOW = 512
NSA_QBLOCK = 32
NSA_FORCE = 1e4
ROPE_THETA = 500000.0
ROPE_DIMS = NSA_DH // 4

PEER_NKEYS = 128
PEER_EXPERTS = PEER_NKEYS * PEER_NKEYS
PEER_HEADS = 8
PEER_QDIM = 256
PEER_TOPK = 16
PEER_TOKEN_BLOCK = 128

kernel_name = "hybrid_ret_gdn_nsa_peer_trunk"


def rms_norm(x, gain):
    xf = x.astype(f32)
    y = xf * lax.rsqrt(jnp.mean(xf * xf, axis=-1, keepdims=True) + NORM_EPS)
    return (y * gain.astype(f32)).astype(x.dtype)


def l2_norm(x):
    xf = x.astype(f32)
    return xf * lax.rsqrt(jnp.sum(xf * xf, axis=-1, keepdims=True) + NORM_EPS)


def rotary(x, pos, rot_dims, theta):
    half = rot_dims // 2
    inv_freq = jnp.power(theta, -jnp.arange(half, dtype=f32) / half)
    ang = pos.astype(f32)[:, None] * inv_freq[None, :]
    cos, sin = jnp.cos(ang), jnp.sin(ang)
    xf = x.astype(f32)
    x1, x2 = xf[..., :half], xf[..., half:rot_dims]
    out = jnp.concatenate([x1 * cos - x2 * sin, x2 * cos + x1 * sin, xf[..., rot_dims:]], axis=-1)
    return out.astype(x.dtype)


def masked_softmax(s, mask):
    s = jnp.where(mask, s.astype(f32), -1e30)
    m = jnp.max(s, axis=-1, keepdims=True)
    e = jnp.where(mask, jnp.exp(s - m), 0.0)
    return e / jnp.maximum(jnp.sum(e, axis=-1, keepdims=True), 1e-30)


def to_heads(z, n_heads):
    B, T, _ = z.shape
    return z.reshape(B, T, n_heads, -1).transpose(0, 2, 1, 3)


def causal_depthwise_conv(x, w):
    K, C = w.shape
    return lax.conv_general_dilated(x, w.astype(x.dtype)[:, None, :], window_strides=(1,),
                                    padding=[(K - 1, 0)], dimension_numbers=('NWC', 'WIO', 'NWC'),
                                    feature_group_count=C)


def retention_mixer(h, w_in, out_gain, w_out):
    B, T, D = h.shape
    H, dk, dv, C = RET_HEADS, RET_DK, RET_DV, RET_CHUNK
    n_chunks = T // C
    q, k, v, g = jnp.split(h @ w_in, 4, axis=-1)
    pos = jnp.arange(T)
    q = rotary(to_heads(q, H), pos, dk, RET_THETA)
    k = rotary(to_heads(k, H), pos, dk, RET_THETA) * (dk ** -0.5)
    v = to_heads(v, H)

    def chunks(z):
        return z.astype(f32).reshape(B, H, n_chunks, C, -1).transpose(2, 0, 1, 3, 4)

    log_gamma = jnp.log1p(-jnp.power(2.0, -5.0 - jnp.arange(H, dtype=f32)))
    idx = jnp.arange(C, dtype=f32)
    rel = idx[:, None] - idx[None, :]
    intra = jnp.where(rel >= 0, jnp.exp(log_gamma[:, None, None] * jnp.maximum(rel, 0.0)), 0.0)
    q_decay = jnp.exp(log_gamma[:, None] * (idx + 1.0))[:, :, None]
    k_decay = jnp.exp(log_gamma[:, None] * (C - 1.0 - idx))[:, :, None]
    chunk_decay = jnp.exp(log_gamma * C)[:, None, None]

    def step(state, qkv):
        qc, kc, vc = qkv
        s = jnp.einsum('bhid,bhjd->bhij', qc, kc) * intra
        o = (jnp.einsum('bhij,bhjv->bhiv', s, vc)
             + jnp.einsum('bhid,bhdv->bhiv', qc, state) * q_decay)
        state = state * chunk_decay + jnp.einsum('bhjd,bhjv->bhdv', kc * k_decay, vc)
        return state, o

    state0 = jnp.zeros((B, H, dk, dv), f32)
    _, o = lax.scan(step, state0, (chunks(q), chunks(k), chunks(v)))
    o = o.transpose(1, 0, 3, 2, 4).reshape(B, T, H, dv)
    o = rms_norm(o, out_gain.reshape(H, dv)).reshape(B, T, H * dv).astype(h.dtype)
    return (jax.nn.silu(g) * o) @ w_out


def gated_deltanet_mixer(h, w_in, conv_w, a_log, dt_bias, out_gain, w_out):
    B, T, D = h.shape
    H, dk, dv, C = GDN_HEADS, GDN_DK, GDN_DV, GDN_CHUNK
    n_qk, n_v = H * dk, H * dv
    n_conv = 2 * n_qk + n_v
    n = T // C
    proj = h @ w_in
    qkv = jax.nn.silu(causal_depthwise_conv(proj[..., :n_conv], conv_w))
    z = proj[..., n_conv:n_conv + n_v]
    beta = jax.nn.sigmoid(proj[..., n_conv + n_v:n_conv + n_v + H].astype(f32))
    a = proj[..., n_conv + n_v + H:].astype(f32)
    g = -jnp.exp(a_log.astype(f32)) * jax.nn.softplus(a + dt_bias.astype(f32))

    def chunks(zz):
        return zz.reshape(B, H, n, C, -1)

    q = chunks(l2_norm(to_heads(qkv[..., :n_qk], H)) * (dk ** -0.5))
    k = chunks(l2_norm(to_heads(qkv[..., n_qk:2 * n_qk], H)))
    v = chunks(to_heads(qkv[..., 2 * n_qk:], H).astype(f32))
    beta = beta.transpose(0, 2, 1).reshape(B, H, n, C)
    g_cum = jnp.cumsum(g.transpose(0, 2, 1).reshape(B, H, n, C), axis=-1)

    tril = jnp.tril(jnp.ones((C, C), bool))
    strict = jnp.tril(jnp.ones((C, C), f32), -1)
    diff = g_cum[..., :, None] - g_cum[..., None, :]
    decay = jnp.where(tril, jnp.exp(jnp.minimum(diff, 0.0)), 0.0)
    k_beta = k * beta[..., None]
    lower = jnp.einsum('bhnid,bhnjd->bhnij', k_beta, k) * decay * strict
    eye = jnp.eye(C, dtype=f32)
    t_mat = lax.linalg.triangular_solve(eye + lower, jnp.broadcast_to(eye, lower.shape),
                                        left_side=True, lower=True)
    u = jnp.einsum('bhnij,bhnjv->bhniv', t_mat, v * beta[..., None])
    w = jnp.einsum('bhnij,bhnjd->bhnid', t_mat, k_beta * jnp.exp(g_cum)[..., None])
    qk = jnp.einsum('bhnid,bhnjd->bhnij', q, k) * decay
    q_dec = q * jnp.exp(g_cum)[..., None]
    g_last = g_cum[..., -1]
    k_dec = k * jnp.exp(g_last[..., None] - g_cum)[..., None]

    def mv(zz):
        return jnp.moveaxis(zz, 2, 0)

    def step(S, inp):
        qk_i, q_dec_i, k_dec_i, u_i, w_i, g_last_i = inp
        v_new = u_i - jnp.einsum('bhcd,bhdv->bhcv', w_i, S)
        o = jnp.einsum('bhcd,bhdv->bhcv', q_dec_i, S) + jnp.einsum('bhij,bhjv->bhiv', qk_i, v_new)
        S = S * jnp.exp(g_last_i)[..., None, None] + jnp.einsum('bhcd,bhcv->bhdv', k_dec_i, v_new)
        return S, o

    S0 = jnp.zeros((B, H, dk, dv), f32)
    _, o = lax.scan(step, S0, (mv(qk), mv(q_dec), mv(k_dec), mv(u), mv(w), jnp.moveaxis(g_last, 2, 0)))
    o = o.transpose(1, 0, 3, 2, 4).reshape(B, T, H, dv)
    o = rms_norm(o, out_gain) * jax.nn.silu(z.reshape(B, T, H, dv).astype(f32))
    return o.reshape(B, T, H * dv).astype(h.dtype) @ w_out


def compress_blocks(zz, idx, pe, w1, w2):
    blocks = zz[:, :, idx] + pe
    flat = blocks.reshape(blocks.shape[0], blocks.shape[1], blocks.shape[2], -1)
    return jax.nn.gelu(flat @ w1, approximate=False) @ w2


def nsa_mixer(h, w_in, q_gain, kc_gain, ks_gain, kw_gain, pe_k, w1_k, w2_k, pe_v, w1_v, w2_v, w_out):
    B, T, D = h.shape
    H, G, Hg, dh = NSA_HEADS, NSA_GROUPS, NSA_HPG, NSA_DH
    kvw = G * dh
    L, S_, SEL, W, QB = NSA_CMP_BLOCK, NSA_CMP_STRIDE, NSA_SEL_BLOCK, NSA_WINDOW, NSA_QBLOCK
    n_cmp = (T - L) // S_ + 1
    n_sel = T // SEL
    top_n = min(NSA_TOPN, n_sel)

    proj = h @ w_in
    q = proj[..., :H * dh].reshape(B, T, H, dh)
    kc, vc, ks, vs, kw, vw = [proj[..., H * dh + j * kvw:H * dh + (j + 1) * kvw]
                              .reshape(B, T, G, dh).transpose(0, 2, 1, 3) for j in range(6)]
    gates = jax.nn.sigmoid(proj[..., H * dh + 6 * kvw:].astype(f32))
    gates = gates.reshape(B, T, 3, H).transpose(0, 2, 3, 1).reshape(B, 3, G, Hg, T)

    pos = jnp.arange(T)
    q = rotary(rms_norm(q, q_gain).transpose(0, 2, 1, 3), pos, ROPE_DIMS, ROPE_THETA)
    q = q.reshape(B, G, Hg, T, dh) * (dh ** -0.5)
    ks = rotary(rms_norm(ks, ks_gain), pos, ROPE_DIMS, ROPE_THETA)
    kw = rotary(rms_norm(kw, kw_gain), pos, ROPE_DIMS, ROPE_THETA)

    cmp_start = jnp.arange(n_cmp) * S_
    cmp_end = cmp_start + L - 1
    cmp_idx = cmp_start[:, None] + jnp.arange(L)[None, :]
    k_cmp = rotary(rms_norm(compress_blocks(kc, cmp_idx, pe_k, w1_k, w2_k), kc_gain), cmp_end,
                   ROPE_DIMS, ROPE_THETA)
    v_cmp = compress_blocks(vc, cmp_idx, pe_v, w1_v, w2_v)

    sel_start = jnp.arange(n_sel) * SEL
    overlap = jnp.clip(jnp.minimum(cmp_start[:, None] + L, sel_start[None, :] + SEL)
                       - jnp.maximum(cmp_start[:, None], sel_start[None, :]), 0).astype(f32) / L
    ks_blk = ks.reshape(B, G, n_sel, SEL, dh)
    vs_blk = vs.reshape(B, G, n_sel, SEL, dh)
    kw_pad = jnp.pad(kw, ((0, 0), (0, 0), (W, 0), (0, 0)))
    vw_pad = jnp.pad(vw, ((0, 0), (0, 0), (W, 0), (0, 0)))
    gather_blocks = jax.vmap(jax.vmap(lambda blocks, ix: blocks[ix]))
    blk_ids = jnp.arange(n_sel)

    def query_block(i):
        start = i * QB
        tq = start + jnp.arange(QB)
        qi = lax.dynamic_slice_in_dim(q, start, QB, axis=3)
        gi = lax.dynamic_slice_in_dim(gates, start, QB, axis=4)
        s = jnp.einsum('bghqd,bgcd->bghqc', qi, k_cmp)
        p_cmp = masked_softmax(s, cmp_end[None, :] <= tq[:, None])
        o_cmp = jnp.einsum('bghqc,bgcd->bghqd', p_cmp, v_cmp)
        imp = jnp.einsum('bghqc,cj->bgqj', p_cmp, overlap)
        cur = tq // SEL
        forced = ((blk_ids[None, :] == 0) | (blk_ids[None, :] == cur[:, None])
                  | (blk_ids[None, :] == cur[:, None] - 1))
        valid = blk_ids[None, :] <= cur[:, None]
        score = jnp.where(forced, NSA_FORCE, jnp.where(valid, imp, -NSA_FORCE))
        _, sel = lax.top_k(score, top_n)
        k_sel = gather_blocks(ks_blk, sel).reshape(B, G, QB, top_n * SEL, dh)
        v_sel = gather_blocks(vs_blk, sel).reshape(B, G, QB, top_n * SEL, dh)
        kpos = (sel[..., None] * SEL + jnp.arange(SEL)).reshape(B, G, QB, top_n * SEL)
        s = jnp.einsum('bghqd,bgqkd->bghqk', qi, k_sel)
        p = masked_softmax(s, (kpos <= tq[None, None, :, None])[:, :, None])
        o_slc = jnp.einsum('bghqk,bgqkd->bghqd', p, v_sel)
        kwin = lax.dynamic_slice_in_dim(kw_pad, start, W + QB, axis=2)
        vwin = lax.dynamic_slice_in_dim(vw_pad, start, W + QB, axis=2)
        wpos = start - W + jnp.arange(W + QB)
        dpos = tq[:, None] - wpos[None, :]
        s = jnp.einsum('bghqd,bgkd->bghqk', qi, kwin)
        p = masked_softmax(s, (dpos >= 0) & (dpos < W) & (wpos[None, :] >= 0))
        o_win = jnp.einsum('bghqk,bgkd->bghqd', p, vwin)
        o = (gi[:, 0][..., None] * o_cmp + gi[:, 1][..., None] * o_slc + gi[:, 2][..., None] * o_win)
        return o.astype(h.dtype)

    o = lax.map(query_block, jnp.arange(T // QB))
    o = o.transpose(1, 0, 4, 2, 3, 5).reshape(B, T, H * dh)
    return o @ w_out


def peer_ffn(h, w_q, keys1, keys2, u, v):
    B, T, D = h.shape
    TB, PH, K, half = PEER_TOKEN_BLOCK, PEER_HEADS, PEER_TOPK, PEER_QDIM // 2

    def block(xb):
        q = (xb @ w_q).astype(f32).reshape(TB, PH, PEER_QDIM)
        s1 = jnp.einsum('thd,kd->thk', q[..., :half], keys1.astype(f32))
        s2 = jnp.einsum('thd,kd->thk', q[..., half:], keys2.astype(f32))
        v1, i1 = lax.top_k(s1, K)
        v2, i2 = lax.top_k(s2, K)
        cand = (v1[..., :, None] + v2[..., None, :]).reshape(TB, PH, K * K)
        cid = (i1[..., :, None] * PEER_NKEYS + i2[..., None, :]).reshape(TB, PH, K * K)
        sc, j = lax.top_k(cand, K)
        eid = jnp.take_along_axis(cid, j, axis=-1)
        gate = jax.nn.softmax(sc, axis=-1)
        act = jax.nn.gelu(jnp.einsum('thkd,td->thk', u[eid], xb).astype(f32), approximate=False)
        return jnp.einsum('thk,thkd->td', (gate * act).astype(xb.dtype), v[eid])

    return lax.map(block, h.reshape(-1, TB, D)).reshape(B, T, D)


def _normal(key, shape, scale):
    return jax.random.normal(key, shape, f32) * scale


def _gain(key, n):
    return 1.0 + 0.02 * jax.random.normal(key, (n,), f32)


def _make_retention(key):
    k = jax.random.split(key, 3)
    return [("ret_w_in", _normal(k[0], (D_MODEL, 4 * D_MODEL), D_MODEL ** -0.5)),
            ("ret_out_gain", _gain(k[1], RET_HEADS * RET_DV)),
            ("ret_w_out", _normal(k[2], (RET_HEADS * RET_DV, D_MODEL), (RET_HEADS * RET_DV) ** -0.5))]


def _make_gdn(key):
    k = jax.random.split(key, 6)
    n_cols = 2 * GDN_HEADS * GDN_DK + 2 * GDN_HEADS * GDN_DV + 2 * GDN_HEADS
    n_conv = 2 * GDN_HEADS * GDN_DK + GDN_HEADS * GDN_DV
    dt = jnp.exp(jax.random.uniform(k[3], (GDN_HEADS,), f32, math.log(1e-3), math.log(1e-1)))
    return [("gdn_w_in", _normal(k[0], (D_MODEL, n_cols), D_MODEL ** -0.5)),
            ("gdn_conv_w", _normal(k[1], (GDN_CONV, n_conv), GDN_CONV ** -0.5)),
            ("gdn_a_log", jnp.log(jax.random.uniform(k[2], (GDN_HEADS,), f32, 1.0, 16.0))),
            ("gdn_dt_bias", dt + jnp.log(-jnp.expm1(-dt))),
            ("gdn_out_gain", _gain(k[4], GDN_DV)),
            ("gdn_w_out", _normal(k[5], (GDN_HEADS * GDN_DV, D_MODEL), (GDN_HEADS * GDN_DV) ** -0.5))]


def _make_nsa(key):
    k = jax.random.split(key, 12)
    n_cols = NSA_HEADS * NSA_DH + 6 * NSA_GROUPS * NSA_DH + 3 * NSA_HEADS
    flat = NSA_CMP_BLOCK * NSA_DH
    return [("nsa_w_in", _normal(k[0], (D_MODEL, n_cols), D_MODEL ** -0.5)),
            ("nsa_q_gain", _gain(k[1], NSA_DH)),
            ("nsa_kc_gain", _gain(k[2], NSA_DH)),
            ("nsa_ks_gain", _gain(k[3], NSA_DH)),
            ("nsa_kw_gain", _gain(k[4], NSA_DH)),
            ("nsa_pe_k", _normal(k[5], (NSA_CMP_BLOCK, NSA_DH), 0.02)),
            ("nsa_w1_k", _normal(k[6], (flat, NSA_CMP_HIDDEN), flat ** -0.5)),
            ("nsa_w2_k", _normal(k[7], (NSA_CMP_HIDDEN, NSA_DH), NSA_CMP_HIDDEN ** -0.5)),
            ("nsa_pe_v", _normal(k[8], (NSA_CMP_BLOCK, NSA_DH), 0.02)),
            ("nsa_w1_v", _normal(k[9], (flat, NSA_CMP_HIDDEN), flat ** -0.5)),
            ("nsa_w2_v", _normal(k[10], (NSA_CMP_HIDDEN, NSA_DH), NSA_CMP_HIDDEN ** -0.5)),
            ("nsa_w_out", _normal(k[11], (NSA_HEADS * NSA_DH, D_MODEL), (NSA_HEADS * NSA_DH) ** -0.5))]


def _make_peer(key):
    k = jax.random.split(key, 5)
    return [("peer_w_q", _normal(k[0], (D_MODEL, PEER_HEADS * PEER_QDIM), D_MODEL ** -0.5)),
            ("peer_keys1", _normal(k[1], (PEER_NKEYS, PEER_QDIM // 2), (PEER_QDIM // 2) ** -0.5)),
            ("peer_keys2", _normal(k[2], (PEER_NKEYS, PEER_QDIM // 2), (PEER_QDIM // 2) ** -0.5)),
            ("peer_u", _normal(k[3], (PEER_EXPERTS, D_MODEL), D_MODEL ** -0.5)),
            ("peer_v", _normal(k[4], (PEER_EXPERTS, D_MODEL), (PEER_HEADS * PEER_TOPK) ** -0.5))]


def setup_inputs(seed: int = 0) -> dict:
    key = jax.random.key(seed)
    k_x, k_layers = jax.random.split(key)
    layer_keys = jax.random.split(k_layers, DEPTH)
    inputs = {"x": jax.random.normal(k_x, (BATCH, SEQ, D_MODEL), f32)}
    makers = (_make_retention, _make_gdn, _make_nsa)
    for i in range(DEPTH):
        ka, km, kf, kp = jax.random.split(layer_keys[i], 4)
        inputs[f"l{i}_attn_norm"] = _gain(ka, D_MODEL)
        for name, arr in makers[i % MIXER_CYCLE](km):
            inputs[f"l{i}_{name}"] = arr
        inputs[f"l{i}_ffn_norm"] = _gain(kf, D_MODEL)
        for name, arr in _make_peer(kp):
            inputs[f"l{i}_{name}"] = arr
    return inputs


def reference(x,
              l0_attn_norm, l0_ret_w_in, l0_ret_out_gain, l0_ret_w_out,
              l0_ffn_norm, l0_peer_w_q, l0_peer_keys1, l0_peer_keys2, l0_peer_u, l0_peer_v,
              l1_attn_norm, l1_gdn_w_in, l1_gdn_conv_w, l1_gdn_a_log, l1_gdn_dt_bias, l1_gdn_out_gain, l1_gdn_w_out,
              l1_ffn_norm, l1_peer_w_q, l1_peer_keys1, l1_peer_keys2, l1_peer_u, l1_peer_v,
              l2_attn_norm, l2_nsa_w_in, l2_nsa_q_gain, l2_nsa_kc_gain, l2_nsa_ks_gain, l2_nsa_kw_gain,
              l2_nsa_pe_k, l2_nsa_w1_k, l2_nsa_w2_k, l2_nsa_pe_v, l2_nsa_w1_v, l2_nsa_w2_v, l2_nsa_w_out,
              l2_ffn_norm, l2_peer_w_q, l2_peer_keys1, l2_peer_keys2, l2_peer_u, l2_peer_v,
              l3_attn_norm, l3_ret_w_in, l3_ret_out_gain, l3_ret_w_out,
              l3_ffn_norm, l3_peer_w_q, l3_peer_keys1, l3_peer_keys2, l3_peer_u, l3_peer_v):
    mixers = (retention_mixer, gated_deltanet_mixer, nsa_mixer)
    attn_norms = (l0_attn_norm, l1_attn_norm, l2_attn_norm, l3_attn_norm)
    mixer_args = (
        (l0_ret_w_in, l0_ret_out_gain, l0_ret_w_out),
        (l1_gdn_w_in, l1_gdn_conv_w, l1_gdn_a_log, l1_gdn_dt_bias, l1_gdn_out_gain, l1_gdn_w_out),
        (l2_nsa_w_in, l2_nsa_q_gain, l2_nsa_kc_gain, l2_nsa_ks_gain, l2_nsa_kw_gain,
         l2_nsa_pe_k, l2_nsa_w1_k, l2_nsa_w2_k, l2_nsa_pe_v, l2_nsa_w1_v, l2_nsa_w2_v, l2_nsa_w_out),
        (l3_ret_w_in, l3_ret_out_gain, l3_ret_w_out),
    )
    ffn_norms = (l0_ffn_norm, l1_ffn_norm, l2_ffn_norm, l3_ffn_norm)
    peer_args = (
        (l0_peer_w_q, l0_peer_keys1, l0_peer_keys2, l0_peer_u, l0_peer_v),
        (l1_peer_w_q, l1_peer_keys1, l1_peer_keys2, l1_peer_u, l1_peer_v),
        (l2_peer_w_q, l2_peer_keys1, l2_peer_keys2, l2_peer_u, l2_peer_v),
        (l3_peer_w_q, l3_peer_keys1, l3_peer_keys2, l3_peer_u, l3_peer_v),
    )
    for i in range(DEPTH):
        x = x + mixers[i % MIXER_CYCLE](rms_norm(x, attn_norms[i]), *mixer_args[i])
        x = x + peer_ffn(rms_norm(x, ffn_norms[i]), *peer_args[i])
    return x
```

```python
import functools
import math

import jax
import jax.numpy as jnp
from jax import lax
from jax.experimental import pallas as pl
from jax.experimental.pallas import tpu as pltpu

D_MODEL = 1024
DEPTH = 4
f32 = jnp.float32
bf16 = jnp.bfloat16
NORM_EPS = 1e-6
MIXER_CYCLE = 3

RET_HEADS = 8
RET_DK = D_MODEL // RET_HEADS
RET_DV = D_MODEL // RET_HEADS
RET_CHUNK = 128
RET_THETA = 10000.0

GDN_HEADS = 8
GDN_DK = D_MODEL // GDN_HEADS
GDN_DV = D_MODEL // GDN_HEADS
GDN_CONV = 4
GDN_CHUNK = 64

NSA_HEADS = 16
NSA_GROUPS = 4
NSA_HPG = NSA_HEADS // NSA_GROUPS
NSA_DH = D_MODEL // NSA_HEADS
NSA_CMP_BLOCK = 32
NSA_CMP_STRIDE = 16
NSA_CMP_HIDDEN = 256
NSA_SEL_BLOCK = 64
NSA_TOPN = 16
NSA_WINDOW = 512
NSA_QBLOCK = 32
NSA_FORCE = 1e4
ROPE_THETA = 500000.0
ROPE_DIMS = NSA_DH // 4

PEER_NKEYS = 128
PEER_EXPERTS = PEER_NKEYS * PEER_NKEYS
PEER_HEADS = 8
PEER_QDIM = 256
PEER_TOPK = 16
PEER_TOKEN_BLOCK = 128


def _mm_kernel(a_ref, b_ref, o_ref):
    o_ref[...] = jnp.dot(a_ref[...].astype(bf16), b_ref[...].astype(bf16),
                         preferred_element_type=f32)


def _mm(a, b, tm=512, tn=1024):
    M, K = a.shape
    _, N = b.shape
    tn = min(tn, N)
    if N % tn:
        tn = N
    return pl.pallas_call(
        _mm_kernel,
        grid=(M // tm, N // tn),
        in_specs=[pl.BlockSpec((tm, K), lambda i, j: (i, 0)),
                  pl.BlockSpec((K, tn), lambda i, j: (0, j))],
        out_specs=pl.BlockSpec((tm, tn), lambda i, j: (i, j)),
        out_shape=jax.ShapeDtypeStruct((M, N), f32),
        compiler_params=pltpu.CompilerParams(
            dimension_semantics=("parallel", "parallel"),
            vmem_limit_bytes=48 * 1024 * 1024),
    )(a, b)


def _proj(h, w):
    B, T, D = h.shape
    return _mm(h.reshape(B * T, D), w).reshape(B, T, -1)


def rms_norm(x, gain):
    xf = x.astype(f32)
    y = xf * lax.rsqrt(jnp.mean(xf * xf, axis=-1, keepdims=True) + NORM_EPS)
    return (y * gain.astype(f32)).astype(x.dtype)


def l2_norm(x):
    xf = x.astype(f32)
    return xf * lax.rsqrt(jnp.sum(xf * xf, axis=-1, keepdims=True) + NORM_EPS)


def rotary(x, pos, rot_dims, theta):
    half = rot_dims // 2
    inv_freq = jnp.power(theta, -jnp.arange(half, dtype=f32) / half)
    ang = pos.astype(f32)[:, None] * inv_freq[None, :]
    cos, sin = jnp.cos(ang), jnp.sin(ang)
    xf = x.astype(f32)
    x1, x2 = xf[..., :half], xf[..., half:rot_dims]
    out = jnp.concatenate([x1 * cos - x2 * sin, x2 * cos + x1 * sin, xf[..., rot_dims:]], axis=-1)
    return out.astype(x.dtype)


def masked_softmax(s, mask):
    s = jnp.where(mask, s.astype(f32), -1e30)
    m = jnp.max(s, axis=-1, keepdims=True)
    e = jnp.where(mask, jnp.exp(s - m), 0.0)
    return e / jnp.maximum(jnp.sum(e, axis=-1, keepdims=True), 1e-30)


def to_heads(z, n_heads):
    B, T, _ = z.shape
    return z.reshape(B, T, n_heads, -1).transpose(0, 2, 1, 3)


def causal_depthwise_conv(x, w):
    K, C = w.shape
    return lax.conv_general_dilated(x, w.astype(x.dtype)[:, None, :], window_strides=(1,),
                                    padding=[(K - 1, 0)], dimension_numbers=('NWC', 'WIO', 'NWC'),
                                    feature_group_count=C)


def retention_mixer(h, w_in, out_gain, w_out):
    B, T, D = h.shape
    H, dk, dv, C = RET_HEADS, RET_DK, RET_DV, RET_CHUNK
    n_chunks = T // C
    q, k, v, g = jnp.split(_proj(h, w_in), 4, axis=-1)
    pos = jnp.arange(T)
    q = rotary(to_heads(q, H), pos, dk, RET_THETA)
    k = rotary(to_heads(k, H), pos, dk, RET_THETA) * (dk ** -0.5)
    v = to_heads(v, H)

    def chunks(z):
        return z.astype(f32).reshape(B, H, n_chunks, C, -1).transpose(2, 0, 1, 3, 4)

    log_gamma = jnp.log1p(-jnp.power(2.0, -5.0 - jnp.arange(H, dtype=f32)))
    idx = jnp.arange(C, dtype=f32)
    rel = idx[:, None] - idx[None, :]
    intra = jnp.where(rel >= 0, jnp.exp(log_gamma[:, None, None] * jnp.maximum(rel, 0.0)), 0.0)
    q_decay = jnp.exp(log_gamma[:, None] * (idx + 1.0))[:, :, None]
    k_decay = jnp.exp(log_gamma[:, None] * (C - 1.0 - idx))[:, :, None]
    chunk_decay = jnp.exp(log_gamma * C)[:, None, None]

    def step(state, qkv):
        qc, kc, vc = qkv
        s = jnp.einsum('bhid,bhjd->bhij', qc, kc) * intra
        o = (jnp.einsum('bhij,bhjv->bhiv', s, vc)
             + jnp.einsum('bhid,bhdv->bhiv', qc, state) * q_decay)
        state = state * chunk_decay + jnp.einsum('bhjd,bhjv->bhdv', kc * k_decay, vc)
        return state, o

    state0 = jnp.zeros((B, H, dk, dv), f32)
    _, o = lax.scan(step, state0, (chunks(q), chunks(k), chunks(v)))
    o = o.transpose(1, 0, 3, 2, 4).reshape(B, T, H, dv)
    o = rms_norm(o, out_gain.reshape(H, dv)).reshape(B, T, H * dv).astype(h.dtype)
    return _proj(jax.nn.silu(g) * o, w_out)


def gated_deltanet_mixer(h, w_in, conv_w, a_log, dt_bias, out_gain, w_out):
    B, T, D = h.shape
    H, dk, dv, C = GDN_HEADS, GDN_DK, GDN_DV, GDN_CHUNK
    n_qk, n_v = H * dk, H * dv
    n_conv = 2 * n_qk + n_v
    n = T // C
    proj = jnp.dot(h, w_in)
    qkv = jax.nn.silu(causal_depthwise_conv(proj[..., :n_conv], conv_w))
    z = proj[..., n_conv:n_conv + n_v]
    beta = jax.nn.sigmoid(proj[..., n_conv + n_v:n_conv + n_v + H].astype(f32))
    a = proj[..., n_conv + n_v + H:].astype(f32)
    g = -jnp.exp(a_log.astype(f32)) * jax.nn.softplus(a + dt_bias.astype(f32))

    def chunks(zz):
        return zz.reshape(B, H, n, C, -1)

    q = chunks(l2_norm(to_heads(qkv[..., :n_qk], H)) * (dk ** -0.5))
    k = chunks(l2_norm(to_heads(qkv[..., n_qk:2 * n_qk], H)))
    v = chunks(to_heads(qkv[..., 2 * n_qk:], H).astype(f32))
    beta = beta.transpose(0, 2, 1).reshape(B, H, n, C)
    g_cum = jnp.cumsum(g.transpose(0, 2, 1).reshape(B, H, n, C), axis=-1)

    tril = jnp.tril(jnp.ones((C, C), bool))
    strict = jnp.tril(jnp.ones((C, C), f32), -1)
    diff = g_cum[..., :, None] - g_cum[..., None, :]
    decay = jnp.where(tril, jnp.exp(jnp.minimum(diff, 0.0)), 0.0)
    k_beta = k * beta[..., None]
    lower = jnp.einsum('bhnid,bhnjd->bhnij', k_beta, k) * decay * strict
    eye = jnp.eye(C, dtype=f32)
    t_mat = lax.linalg.triangular_solve(eye + lower, jnp.broadcast_to(eye, lower.shape),
                                        left_side=True, lower=True)
    u = jnp.einsum('bhnij,bhnjv->bhniv', t_mat, v * beta[..., None])
    w = jnp.einsum('bhnij,bhnjd->bhnid', t_mat, k_beta * jnp.exp(g_cum)[..., None])
    qk = jnp.einsum('bhnid,bhnjd->bhnij', q, k) * decay
    q_dec = q * jnp.exp(g_cum)[..., None]
    g_last = g_cum[..., -1]
    k_dec = k * jnp.exp(g_last[..., None] - g_cum)[..., None]

    def mv(zz):
        return jnp.moveaxis(zz, 2, 0)

    def step(S, inp):
        qk_i, q_dec_i, k_dec_i, u_i, w_i, g_last_i = inp
        v_new = u_i - jnp.einsum('bhcd,bhdv->bhcv', w_i, S)
        o = jnp.einsum('bhcd,bhdv->bhcv', q_dec_i, S) + jnp.einsum('bhij,bhjv->bhiv', qk_i, v_new)
        S = S * jnp.exp(g_last_i)[..., None, None] + jnp.einsum('bhcd,bhcv->bhdv', k_dec_i, v_new)
        return S, o

    S0 = jnp.zeros((B, H, dk, dv), f32)
    _, o = lax.scan(step, S0, (mv(qk), mv(q_dec), mv(k_dec), mv(u), mv(w), jnp.moveaxis(g_last, 2, 0)))
    o = o.transpose(1, 0, 3, 2, 4).reshape(B, T, H, dv)
    o = rms_norm(o, out_gain) * jax.nn.silu(z.reshape(B, T, H, dv).astype(f32))
    return _proj(o.reshape(B, T, H * dv).astype(h.dtype), w_out)


def compress_blocks(zz, idx, pe, w1, w2):
    blocks = zz[:, :, idx] + pe
    flat = blocks.reshape(blocks.shape[0], blocks.shape[1], blocks.shape[2], -1)
    return jax.nn.gelu(flat @ w1, approximate=False) @ w2


def nsa_mixer(h, w_in, q_gain, kc_gain, ks_gain, kw_gain, pe_k, w1_k, w2_k, pe_v, w1_v, w2_v, w_out):
    B, T, D = h.shape
    H, G, Hg, dh = NSA_HEADS, NSA_GROUPS, NSA_HPG, NSA_DH
    kvw = G * dh
    L, S_, SEL, W, QB = NSA_CMP_BLOCK, NSA_CMP_STRIDE, NSA_SEL_BLOCK, NSA_WINDOW, NSA_QBLOCK
    n_cmp = (T - L) // S_ + 1
    n_sel = T // SEL
    top_n = min(NSA_TOPN, n_sel)

    proj = jnp.dot(h, w_in)
    q = proj[..., :H * dh].reshape(B, T, H, dh)
    kc, vc, ks, vs, kw, vw = [proj[..., H * dh + j * kvw:H * dh + (j + 1) * kvw]
                              .reshape(B, T, G, dh).transpose(0, 2, 1, 3) for j in range(6)]
    gates = jax.nn.sigmoid(proj[..., H * dh + 6 * kvw:].astype(f32))
    gates = gates.reshape(B, T, 3, H).transpose(0, 2, 3, 1).reshape(B, 3, G, Hg, T)

    pos = jnp.arange(T)
    q = rotary(rms_norm(q, q_gain).transpose(0, 2, 1, 3), pos, ROPE_DIMS, ROPE_THETA)
    q = q.reshape(B, G, Hg, T, dh) * (dh ** -0.5)
    ks = rotary(rms_norm(ks, ks_gain), pos, ROPE_DIMS, ROPE_THETA)
    kw = rotary(rms_norm(kw, kw_gain), pos, ROPE_DIMS, ROPE_THETA)

    cmp_start = jnp.arange(n_cmp) * S_
    cmp_end = cmp_start + L - 1
    cmp_idx = cmp_start[:, None] + jnp.arange(L)[None, :]
    k_cmp = rotary(rms_norm(compress_blocks(kc, cmp_idx, pe_k, w1_k, w2_k), kc_gain), cmp_end,
                   ROPE_DIMS, ROPE_THETA)
    v_cmp = compress_blocks(vc, cmp_idx, pe_v, w1_v, w2_v)

    sel_start = jnp.arange(n_sel) * SEL
    overlap = jnp.clip(jnp.minimum(cmp_start[:, None] + L, sel_start[None, :] + SEL)
                       - jnp.maximum(cmp_start[:, None], sel_start[None, :]), 0).astype(f32) / L
    ks_blk = ks.reshape(B, G, n_sel, SEL, dh)
    vs_blk = vs.reshape(B, G, n_sel, SEL, dh)
    kw_pad = jnp.pad(kw, ((0, 0), (0, 0), (W, 0), (0, 0)))
    vw_pad = jnp.pad(vw, ((0, 0), (0, 0), (W, 0), (0, 0)))
    gather_blocks = jax.vmap(jax.vmap(lambda blocks, ix: blocks[ix]))
    blk_ids = jnp.arange(n_sel)

    def query_block(i):
        start = i * QB
        tq = start + jnp.arange(QB)
        qi = lax.dynamic_slice_in_dim(q, start, QB, axis=3)
        gi = lax.dynamic_slice_in_dim(gates, start, QB, axis=4)
        s = jnp.einsum('bghqd,bgcd->bghqc', qi, k_cmp)
        p_cmp = masked_softmax(s, cmp_end[None, :] <= tq[:, None])
        o_cmp = jnp.einsum('bghqc,bgcd->bghqd', p_cmp, v_cmp)
        imp = jnp.einsum('bghqc,cj->bgqj', p_cmp, overlap)
        cur = tq // SEL
        forced = ((blk_ids[None, :] == 0) | (blk_ids[None, :] == cur[:, None])
                  | (blk_ids[None, :] == cur[:, None] - 1))
        valid = blk_ids[None, :] <= cur[:, None]
        score = jnp.where(forced, NSA_FORCE, jnp.where(valid, imp, -NSA_FORCE))
        _, sel = lax.top_k(score, top_n)
        k_sel = gather_blocks(ks_blk, sel).reshape(B, G, QB, top_n * SEL, dh)
        v_sel = gather_blocks(vs_blk, sel).reshape(B, G, QB, top_n * SEL, dh)
        kpos = (sel[..., None] * SEL + jnp.arange(SEL)).reshape(B, G, QB, top_n * SEL)
        s = jnp.einsum('bghqd,bgqkd->bghqk', qi, k_sel)
        p = masked_softmax(s, (kpos <= tq[None, None, :, None])[:, :, None])
        o_slc = jnp.einsum('bghqk,bgqkd->bghqd', p, v_sel)
        kwin = lax.dynamic_slice_in_dim(kw_pad, start, W + QB, axis=2)
        vwin = lax.dynamic_slice_in_dim(vw_pad, start, W + QB, axis=2)
        wpos = start - W + jnp.arange(W + QB)
        dpos = tq[:, None] - wpos[None, :]
        s = jnp.einsum('bghqd,bgkd->bghqk', qi, kwin)
        p = masked_softmax(s, (dpos >= 0) & (dpos < W) & (wpos[None, :] >= 0))
        o_win = jnp.einsum('bghqk,bgkd->bghqd', p, vwin)
        o = (gi[:, 0][..., None] * o_cmp + gi[:, 1][..., None] * o_slc + gi[:, 2][..., None] * o_win)
        return o.astype(h.dtype)

    o = lax.map(query_block, jnp.arange(T // QB))
    o = o.transpose(1, 0, 4, 2, 3, 5).reshape(B, T, H * dh)
    return _proj(o, w_out)


def peer_ffn(h, w_q, keys1, keys2, u, v):
    B, T, D = h.shape
    TB, PH, K, half = PEER_TOKEN_BLOCK, PEER_HEADS, PEER_TOPK, PEER_QDIM // 2
    qall = _mm(h.reshape(B * T, D), w_q)

    def block(args):
        xb, qb = args
        q = qb.astype(f32).reshape(TB, PH, PEER_QDIM)
        s1 = jnp.einsum('thd,kd->thk', q[..., :half], keys1.astype(f32))
        s2 = jnp.einsum('thd,kd->thk', q[..., half:], keys2.astype(f32))
        v1, i1 = lax.top_k(s1, K)
        v2, i2 = lax.top_k(s2, K)
        cand = (v1[..., :, None] + v2[..., None, :]).reshape(TB, PH, K * K)
        cid = (i1[..., :, None] * PEER_NKEYS + i2[..., None, :]).reshape(TB, PH, K * K)
        sc, j = lax.top_k(cand, K)
        eid = jnp.take_along_axis(cid, j, axis=-1)
        gate = jax.nn.softmax(sc, axis=-1)
        act = jax.nn.gelu(jnp.einsum('thkd,td->thk', u[eid], xb).astype(f32), approximate=False)
        return jnp.einsum('thk,thkd->td', (gate * act).astype(xb.dtype), v[eid])

    return lax.map(block, (h.reshape(-1, TB, D), qall.reshape(-1, TB, PH * PEER_QDIM))).reshape(B, T, D)


def kernel(x,
           l0_attn_norm, l0_ret_w_in, l0_ret_out_gain, l0_ret_w_out,
           l0_ffn_norm, l0_peer_w_q, l0_peer_keys1, l0_peer_keys2, l0_peer_u, l0_peer_v,
           l1_attn_norm, l1_gdn_w_in, l1_gdn_conv_w, l1_gdn_a_log, l1_gdn_dt_bias, l1_gdn_out_gain, l1_gdn_w_out,
           l1_ffn_norm, l1_peer_w_q, l1_peer_keys1, l1_peer_keys2, l1_peer_u, l1_peer_v,
           l2_attn_norm, l2_nsa_w_in, l2_nsa_q_gain, l2_nsa_kc_gain, l2_nsa_ks_gain, l2_nsa_kw_gain,
           l2_nsa_pe_k, l2_nsa_w1_k, l2_nsa_w2_k, l2_nsa_pe_v, l2_nsa_w1_v, l2_nsa_w2_v, l2_nsa_w_out,
           l2_ffn_norm, l2_peer_w_q, l2_peer_keys1, l2_peer_keys2, l2_peer_u, l2_peer_v,
           l3_attn_norm, l3_ret_w_in, l3_ret_out_gain, l3_ret_w_out,
           l3_ffn_norm, l3_peer_w_q, l3_peer_keys1, l3_peer_keys2, l3_peer_u, l3_peer_v):
    mixers = (retention_mixer, gated_deltanet_mixer, nsa_mixer)
    attn_norms = (l0_attn_norm, l1_attn_norm, l2_attn_norm, l3_attn_norm)
    mixer_args = (
        (l0_ret_w_in, l0_ret_out_gain, l0_ret_w_out),
        (l1_gdn_w_in, l1_gdn_conv_w, l1_gdn_a_log, l1_gdn_dt_bias, l1_gdn_out_gain, l1_gdn_w_out),
        (l2_nsa_w_in, l2_nsa_q_gain, l2_nsa_kc_gain, l2_nsa_ks_gain, l2_nsa_kw_gain,
         l2_nsa_pe_k, l2_nsa_w1_k, l2_nsa_w2_k, l2_nsa_pe_v, l2_nsa_w1_v, l2_nsa_w2_v, l2_nsa_w_out),
        (l3_ret_w_in, l3_ret_out_gain, l3_ret_w_out),
    )
    ffn_norms = (l0_ffn_norm, l1_ffn_norm, l2_ffn_norm, l3_ffn_norm)
    peer_args = (
        (l0_peer_w_q, l0_peer_keys1, l0_peer_keys2, l0_peer_u, l0_peer_v),
        (l1_peer_w_q, l1_peer_keys1, l1_peer_keys2, l1_peer_u, l1_peer_v),
        (l2_peer_w_q, l2_peer_keys1, l2_peer_keys2, l2_peer_u, l2_peer_v),
        (l3_peer_w_q, l3_peer_keys1, l3_peer_keys2, l3_peer_u, l3_peer_v),
    )
    for i in range(DEPTH):
        x = x + mixers[i % MIXER_CYCLE](rms_norm(x, attn_norms[i]), *mixer_args[i])
        x = x + peer_ffn(rms_norm(x, ffn_norms[i]), *peer_args[i])
    return x
```

```python
import functools
import math

import jax
import jax.numpy as jnp
from jax import lax
from jax.experimental import pallas as pl
from jax.experimental.pallas import tpu as pltpu
from jax.experimental.pallas import tpu_sc as plsc

D_MODEL = 1024
DEPTH = 4
f32 = jnp.float32
bf16 = jnp.bfloat16
NORM_EPS = 1e-6
MIXER_CYCLE = 3

RET_HEADS = 8
RET_DK = D_MODEL // RET_HEADS
RET_DV = D_MODEL // RET_HEADS
RET_CHUNK = 128
RET_THETA = 10000.0

GDN_HEADS = 8
GDN_DK = D_MODEL // GDN_HEADS
GDN_DV = D_MODEL // GDN_HEADS
GDN_CONV = 4
GDN_CHUNK = 64

NSA_HEADS = 16
NSA_GROUPS = 4
NSA_HPG = NSA_HEADS // NSA_GROUPS
NSA_DH = D_MODEL // NSA_HEADS
NSA_CMP_BLOCK = 32
NSA_CMP_STRIDE = 16
NSA_CMP_HIDDEN = 256
NSA_SEL_BLOCK = 64
NSA_TOPN = 16
NSA_WINDOW = 512
NSA_QBLOCK = 32
NSA_FORCE = 1e4
ROPE_THETA = 500000.0
ROPE_DIMS = NSA_DH // 4

PEER_NKEYS = 128
PEER_EXPERTS = PEER_NKEYS * PEER_NKEYS
PEER_HEADS = 8
PEER_QDIM = 256
PEER_TOPK = 16
PEER_TOKEN_BLOCK = 128


def _mm_kernel(a_ref, b_ref, o_ref):
    o_ref[...] = jnp.dot(a_ref[...].astype(bf16), b_ref[...].astype(bf16),
                         preferred_element_type=f32)


def _mm(a, b, tm=512, tn=1024):
    M, K = a.shape
    _, N = b.shape
    tn = min(tn, N)
    if N % tn:
        tn = N
    return pl.pallas_call(
        _mm_kernel,
        grid=(M // tm, N // tn),
        in_specs=[pl.BlockSpec((tm, K), lambda i, j: (i, 0)),
                  pl.BlockSpec((K, tn), lambda i, j: (0, j))],
        out_specs=pl.BlockSpec((tm, tn), lambda i, j: (i, j)),
        out_shape=jax.ShapeDtypeStruct((M, N), f32),
        compiler_params=pltpu.CompilerParams(
            dimension_semantics=("parallel", "parallel"),
            vmem_limit_bytes=48 * 1024 * 1024),
    )(a, b)


def _proj(h, w):
    B, T, D = h.shape
    return _mm(h.reshape(B * T, D), w).reshape(B, T, -1)


def rms_norm(x, gain):
    xf = x.astype(f32)
    y = xf * lax.rsqrt(jnp.mean(xf * xf, axis=-1, keepdims=True) + NORM_EPS)
    return (y * gain.astype(f32)).astype(x.dtype)


def l2_norm(x):
    xf = x.astype(f32)
    return xf * lax.rsqrt(jnp.sum(xf * xf, axis=-1, keepdims=True) + NORM_EPS)


def rotary(x, pos, rot_dims, theta):
    half = rot_dims // 2
    inv_freq = jnp.power(theta, -jnp.arange(half, dtype=f32) / half)
    ang = pos.astype(f32)[:, None] * inv_freq[None, :]
    cos, sin = jnp.cos(ang), jnp.sin(ang)
    xf = x.astype(f32)
    x1, x2 = xf[..., :half], xf[..., half:rot_dims]
    out = jnp.concatenate([x1 * cos - x2 * sin, x2 * cos + x1 * sin, xf[..., rot_dims:]], axis=-1)
    return out.astype(x.dtype)


def masked_softmax(s, mask):
    s = jnp.where(mask, s.astype(f32), -1e30)
    m = jnp.max(s, axis=-1, keepdims=True)
    e = jnp.where(mask, jnp.exp(s - m), 0.0)
    return e / jnp.maximum(jnp.sum(e, axis=-1, keepdims=True), 1e-30)


def to_heads(z, n_heads):
    B, T, _ = z.shape
    return z.reshape(B, T, n_heads, -1).transpose(0, 2, 1, 3)


def causal_depthwise_conv(x, w):
    K, C = w.shape
    return lax.conv_general_dilated(x, w.astype(x.dtype)[:, None, :], window_strides=(1,),
                                    padding=[(K - 1, 0)], dimension_numbers=('NWC', 'WIO', 'NWC'),
                                    feature_group_count=C)


def retention_mixer(h, w_in, out_gain, w_out):
    B, T, D = h.shape
    H, dk, dv, C = RET_HEADS, RET_DK, RET_DV, RET_CHUNK
    n_chunks = T // C
    q, k, v, g = jnp.split(_proj(h, w_in), 4, axis=-1)
    pos = jnp.arange(T)
    q = rotary(to_heads(q, H), pos, dk, RET_THETA)
    k = rotary(to_heads(k, H), pos, dk, RET_THETA) * (dk ** -0.5)
    v = to_heads(v, H)

    def chunks(z):
        return z.astype(f32).reshape(B, H, n_chunks, C, -1).transpose(2, 0, 1, 3, 4)

    log_gamma = jnp.log1p(-jnp.power(2.0, -5.0 - jnp.arange(H, dtype=f32)))
    idx = jnp.arange(C, dtype=f32)
    rel = idx[:, None] - idx[None, :]
    intra = jnp.where(rel >= 0, jnp.exp(log_gamma[:, None, None] * jnp.maximum(rel, 0.0)), 0.0)
    q_decay = jnp.exp(log_gamma[:, None] * (idx + 1.0))[:, :, None]
    k_decay = jnp.exp(log_gamma[:, None] * (C - 1.0 - idx))[:, :, None]
    chunk_decay = jnp.exp(log_gamma * C)[:, None, None]

    def step(state, qkv):
        qc, kc, vc = qkv
        s = jnp.einsum('bhid,bhjd->bhij', qc, kc) * intra
        o = (jnp.einsum('bhij,bhjv->bhiv', s, vc)
             + jnp.einsum('bhid,bhdv->bhiv', qc, state) * q_decay)
        state = state * chunk_decay + jnp.einsum('bhjd,bhjv->bhdv', kc * k_decay, vc)
        return state, o

    state0 = jnp.zeros((B, H, dk, dv), f32)
    _, o = lax.scan(step, state0, (chunks(q), chunks(k), chunks(v)))
    o = o.transpose(1, 0, 3, 2, 4).reshape(B, T, H, dv)
    o = rms_norm(o, out_gain.reshape(H, dv)).reshape(B, T, H * dv).astype(h.dtype)
    return _proj(jax.nn.silu(g) * o, w_out)


def gated_deltanet_mixer(h, w_in, conv_w, a_log, dt_bias, out_gain, w_out):
    B, T, D = h.shape
    H, dk, dv, C = GDN_HEADS, GDN_DK, GDN_DV, GDN_CHUNK
    n_qk, n_v = H * dk, H * dv
    n_conv = 2 * n_qk + n_v
    n = T // C
    proj = jnp.dot(h, w_in)
    qkv = jax.nn.silu(causal_depthwise_conv(proj[..., :n_conv], conv_w))
    z = proj[..., n_conv:n_conv + n_v]
    beta = jax.nn.sigmoid(proj[..., n_conv + n_v:n_conv + n_v + H].astype(f32))
    a = proj[..., n_conv + n_v + H:].astype(f32)
    g = -jnp.exp(a_log.astype(f32)) * jax.nn.softplus(a + dt_bias.astype(f32))

    def chunks(zz):
        return zz.reshape(B, H, n, C, -1)

    q = chunks(l2_norm(to_heads(qkv[..., :n_qk], H)) * (dk ** -0.5))
    k = chunks(l2_norm(to_heads(qkv[..., n_qk:2 * n_qk], H)))
    v = chunks(to_heads(qkv[..., 2 * n_qk:], H).astype(f32))
    beta = beta.transpose(0, 2, 1).reshape(B, H, n, C)
    g_cum = jnp.cumsum(g.transpose(0, 2, 1).reshape(B, H, n, C), axis=-1)

    tril = jnp.tril(jnp.ones((C, C), bool))
    strict = jnp.tril(jnp.ones((C, C), f32), -1)
    diff = g_cum[..., :, None] - g_cum[..., None, :]
    decay = jnp.where(tril, jnp.exp(jnp.minimum(diff, 0.0)), 0.0)
    k_beta = k * beta[..., None]
    lower = jnp.einsum('bhnid,bhnjd->bhnij', k_beta, k) * decay * strict
    eye = jnp.eye(C, dtype=f32)
    t_mat = lax.linalg.triangular_solve(eye + lower, jnp.broadcast_to(eye, lower.shape),
                                        left_side=True, lower=True)
    u = jnp.einsum('bhnij,bhnjv->bhniv', t_mat, v * beta[..., None])
    w = jnp.einsum('bhnij,bhnjd->bhnid', t_mat, k_beta * jnp.exp(g_cum)[..., None])
    qk = jnp.einsum('bhnid,bhnjd->bhnij', q, k) * decay
    q_dec = q * jnp.exp(g_cum)[..., None]
    g_last = g_cum[..., -1]
    k_dec = k * jnp.exp(g_last[..., None] - g_cum)[..., None]

    def mv(zz):
        return jnp.moveaxis(zz, 2, 0)

    def step(S, inp):
        qk_i, q_dec_i, k_dec_i, u_i, w_i, g_last_i = inp
        v_new = u_i - jnp.einsum('bhcd,bhdv->bhcv', w_i, S)
        o = jnp.einsum('bhcd,bhdv->bhcv', q_dec_i, S) + jnp.einsum('bhij,bhjv->bhiv', qk_i, v_new)
        S = S * jnp.exp(g_last_i)[..., None, None] + jnp.einsum('bhcd,bhcv->bhdv', k_dec_i, v_new)
        return S, o

    S0 = jnp.zeros((B, H, dk, dv), f32)
    _, o = lax.scan(step, S0, (mv(qk), mv(q_dec), mv(k_dec), mv(u), mv(w), jnp.moveaxis(g_last, 2, 0)))
    o = o.transpose(1, 0, 3, 2, 4).reshape(B, T, H, dv)
    o = rms_norm(o, out_gain) * jax.nn.silu(z.reshape(B, T, H, dv).astype(f32))
    return _proj(o.reshape(B, T, H * dv).astype(h.dtype), w_out)


def compress_blocks(zz, idx, pe, w1, w2):
    blocks = zz[:, :, idx] + pe
    flat = blocks.reshape(blocks.shape[0], blocks.shape[1], blocks.shape[2], -1)
    return jax.nn.gelu(flat @ w1, approximate=False) @ w2


def nsa_mixer(h, w_in, q_gain, kc_gain, ks_gain, kw_gain, pe_k, w1_k, w2_k, pe_v, w1_v, w2_v, w_out):
    B, T, D = h.shape
    H, G, Hg, dh = NSA_HEADS, NSA_GROUPS, NSA_HPG, NSA_DH
    kvw = G * dh
    L, S_, SEL, W, QB = NSA_CMP_BLOCK, NSA_CMP_STRIDE, NSA_SEL_BLOCK, NSA_WINDOW, NSA_QBLOCK
    n_cmp = (T - L) // S_ + 1
    n_sel = T // SEL
    top_n = min(NSA_TOPN, n_sel)

    proj = jnp.dot(h, w_in)
    q = proj[..., :H * dh].reshape(B, T, H, dh)
    kc, vc, ks, vs, kw, vw = [proj[..., H * dh + j * kvw:H * dh + (j + 1) * kvw]
                              .reshape(B, T, G, dh).transpose(0, 2, 1, 3) for j in range(6)]
    gates = jax.nn.sigmoid(proj[..., H * dh + 6 * kvw:].astype(f32))
    gates = gates.reshape(B, T, 3, H).transpose(0, 2, 3, 1).reshape(B, 3, G, Hg, T)

    pos = jnp.arange(T)
    q = rotary(rms_norm(q, q_gain).transpose(0, 2, 1, 3), pos, ROPE_DIMS, ROPE_THETA)
    q = q.reshape(B, G, Hg, T, dh) * (dh ** -0.5)
    ks = rotary(rms_norm(ks, ks_gain), pos, ROPE_DIMS, ROPE_THETA)
    kw = rotary(rms_norm(kw, kw_gain), pos, ROPE_DIMS, ROPE_THETA)

    cmp_start = jnp.arange(n_cmp) * S_
    cmp_end = cmp_start + L - 1
    cmp_idx = cmp_start[:, None] + jnp.arange(L)[None, :]
    k_cmp = rotary(rms_norm(compress_blocks(kc, cmp_idx, pe_k, w1_k, w2_k), kc_gain), cmp_end,
                   ROPE_DIMS, ROPE_THETA)
    v_cmp = compress_blocks(vc, cmp_idx, pe_v, w1_v, w2_v)

    sel_start = jnp.arange(n_sel) * SEL
    overlap = jnp.clip(jnp.minimum(cmp_start[:, None] + L, sel_start[None, :] + SEL)
                       - jnp.maximum(cmp_start[:, None], sel_start[None, :]), 0).astype(f32) / L
    ks_blk = ks.reshape(B, G, n_sel, SEL, dh)
    vs_blk = vs.reshape(B, G, n_sel, SEL, dh)
    kw_pad = jnp.pad(kw, ((0, 0), (0, 0), (W, 0), (0, 0)))
    vw_pad = jnp.pad(vw, ((0, 0), (0, 0), (W, 0), (0, 0)))
    gather_blocks = jax.vmap(jax.vmap(lambda blocks, ix: blocks[ix]))
    blk_ids = jnp.arange(n_sel)

    def query_block(i):
        start = i * QB
        tq = start + jnp.arange(QB)
        qi = lax.dynamic_slice_in_dim(q, start, QB, axis=3)
        gi = lax.dynamic_slice_in_dim(gates, start, QB, axis=4)
        s = jnp.einsum('bghqd,bgcd->bghqc', qi, k_cmp)
        p_cmp = masked_softmax(s, cmp_end[None, :] <= tq[:, None])
        o_cmp = jnp.einsum('bghqc,bgcd->bghqd', p_cmp, v_cmp)
        imp = jnp.einsum('bghqc,cj->bgqj', p_cmp, overlap)
        cur = tq // SEL
        forced = ((blk_ids[None, :] == 0) | (blk_ids[None, :] == cur[:, None])
                  | (blk_ids[None, :] == cur[:, None] - 1))
        valid = blk_ids[None, :] <= cur[:, None]
        score = jnp.where(forced, NSA_FORCE, jnp.where(valid, imp, -NSA_FORCE))
        _, sel = lax.top_k(score, top_n)
        k_sel = gather_blocks(ks_blk, sel).reshape(B, G, QB, top_n * SEL, dh)
        v_sel = gather_blocks(vs_blk, sel).reshape(B, G, QB, top_n * SEL, dh)
        kpos = (sel[..., None] * SEL + jnp.arange(SEL)).reshape(B, G, QB, top_n * SEL)
        s = jnp.einsum('bghqd,bgqkd->bghqk', qi, k_sel)
        p = masked_softmax(s, (kpos <= tq[None, None, :, None])[:, :, None])
        o_slc = jnp.einsum('bghqk,bgqkd->bghqd', p, v_sel)
        kwin = lax.dynamic_slice_in_dim(kw_pad, start, W + QB, axis=2)
        vwin = lax.dynamic_slice_in_dim(vw_pad, start, W + QB, axis=2)
        wpos = start - W + jnp.arange(W + QB)
        dpos = tq[:, None] - wpos[None, :]
        s = jnp.einsum('bghqd,bgkd->bghqk', qi, kwin)
        p = masked_softmax(s, (dpos >= 0) & (dpos < W) & (wpos[None, :] >= 0))
        o_win = jnp.einsum('bghqk,bgkd->bghqd', p, vwin)
        o = (gi[:, 0][..., None] * o_cmp + gi[:, 1][..., None] * o_slc + gi[:, 2][..., None] * o_win)
        return o.astype(h.dtype)

    o = lax.map(query_block, jnp.arange(T // QB))
    o = o.transpose(1, 0, 4, 2, 3, 5).reshape(B, T, H * dh)
    return _proj(o, w_out)


PEER_TM = 256
SC_LANES = 16
SC_WORKERS = 32
SC_ROWS = 32
SC_TOK_BLOCK = 8
PEER_PICKS = PEER_HEADS * PEER_TOPK


def _peer_score_kernel(x_ref, g_ref, wq_ref, k1_ref, k2_ref, xn_ref, s1_ref, s2_ref):
    x = x_ref[...]
    xn = x * lax.rsqrt(jnp.mean(x * x, axis=-1, keepdims=True) + NORM_EPS) * g_ref[...]
    xn_ref[...] = xn
    q = jnp.dot(xn.astype(bf16), wq_ref[...], preferred_element_type=f32)
    half = PEER_QDIM // 2
    k1 = k1_ref[...]
    k2 = k2_ref[...]
    dn = (((1,), (1,)), ((), ()))
    for h in range(PEER_HEADS):
        q1 = q[:, h * PEER_QDIM:h * PEER_QDIM + half].astype(bf16)
        q2 = q[:, h * PEER_QDIM + half:(h + 1) * PEER_QDIM].astype(bf16)
        s1_ref[:, h * PEER_NKEYS:(h + 1) * PEER_NKEYS] = lax.dot_general(q1, k1, dn, preferred_element_type=f32)
        s2_ref[:, h * PEER_NKEYS:(h + 1) * PEER_NKEYS] = lax.dot_general(q2, k2, dn, preferred_element_type=f32)


def _peer_scores(x2, gain, w_q, keys1, keys2):
    N, D = x2.shape
    nq = PEER_HEADS * PEER_QDIM
    ns = PEER_HEADS * PEER_NKEYS
    return pl.pallas_call(
        _peer_score_kernel,
        grid=(N // PEER_TM,),
        in_specs=[pl.BlockSpec((PEER_TM, D), lambda i: (i, 0)),
                  pl.BlockSpec((1, D), lambda i: (0, 0)),
                  pl.BlockSpec((D, nq), lambda i: (0, 0)),
                  pl.BlockSpec(keys1.shape, lambda i: (0, 0)),
                  pl.BlockSpec(keys2.shape, lambda i: (0, 0))],
        out_specs=[pl.BlockSpec((PEER_TM, D), lambda i: (i, 0)),
                   pl.BlockSpec((PEER_TM, ns), lambda i: (i, 0)),
                   pl.BlockSpec((PEER_TM, ns), lambda i: (i, 0))],
        out_shape=[jax.ShapeDtypeStruct((N, D), f32),
                   jax.ShapeDtypeStruct((N, ns), f32),
                   jax.ShapeDtypeStruct((N, ns), f32)],
        compiler_params=pltpu.CompilerParams(
            dimension_semantics=("parallel",), vmem_limit_bytes=48 * 1024 * 1024),
        name="peer_scores",
    )(x2, gain.reshape(1, D), w_q.astype(bf16), keys1.astype(bf16), keys2.astype(bf16))


def _sc_mesh():
    return plsc.VectorSubcoreMesh(core_axis_name="c", subcore_axis_name="s")


def _sc_gather_dot(u, idx, x2):
    N, D = x2.shape
    G = PEER_PICKS // SC_ROWS
    R, L = SC_ROWS, SC_LANES
    tpw = N // SC_WORKERS
    nblk = tpw // SC_TOK_BLOCK
    items = SC_TOK_BLOCK * G
    nch = D // L

    @functools.partial(
        pl.kernel, mesh=_sc_mesh(), compiler_params=pltpu.CompilerParams(needs_layout_passes=False),
        out_type=jax.ShapeDtypeStruct((N * G, R), f32),
        scratch_types=[
            pltpu.VMEM((items, R), jnp.int32),
            pltpu.VMEM((SC_TOK_BLOCK, D), f32),
            pltpu.VMEM((items, R), f32),
            pltpu.VMEM((R, D), f32),
            pltpu.VMEM((R, D), f32),
            pltpu.SemaphoreType.DMA,
            pltpu.SemaphoreType.DMA,
        ],
        name="peer_sc_dot",
    )
    def k(u_hbm, idx_hbm, x_hbm, h_hbm, idx_v, x_v, h_v, rows0, rows1, sem0, sem1):
        wid = lax.axis_index("s") * 2 + lax.axis_index("c")
        rows = (rows0, rows1)
        sems = (sem0, sem1)
        lane = lax.iota(jnp.int32, L)

        def gather(item, b):
            return pltpu.make_async_copy(u_hbm.at[idx_v.at[item]], rows[b], sems[b])

        def compute(item, b):
            tl = item // G
            rv = rows[b]

            def chunk(c, accs):
                off = pl.multiple_of(c * L, L)
                xc = x_v[tl, pl.ds(off, L)]
                return tuple(accs[r] + rv[r, pl.ds(off, L)] * xc for r in range(R))

            accs = lax.fori_loop(0, nch, chunk, tuple(jnp.zeros((L,), f32) for _ in range(R)))
            for half in range(R // L):
                out = jnp.zeros((L,), f32)
                for r in range(L):
                    out = jnp.where(lane == r, jnp.sum(accs[half * L + r]), out)
                h_v[item, pl.ds(half * L, L)] = out

        @pl.loop(0, nblk)
        def _(blk):
            tok0 = wid * tpw + blk * SC_TOK_BLOCK
            it0 = tok0 * G
            pltpu.sync_copy(idx_hbm.at[pl.ds(it0, items)], idx_v)
            pltpu.sync_copy(x_hbm.at[pl.ds(tok0, SC_TOK_BLOCK)], x_v)
            gather(0, 0).start()

            @pl.loop(0, items, step=2)
            def _(it):
                gather(it + 1, 1).start()
                gather(it, 0).wait()
                compute(it, 0)

                @pl.when(it + 2 < items)
                def _():
                    gather(it + 2, 0).start()

                gather(it + 1, 1).wait()
                compute(it + 1, 1)

            pltpu.sync_copy(h_v, h_hbm.at[pl.ds(it0, items)])

    return k(u, idx, x2)


def _sc_gather_wsum(v, idx, w, N):
    D = v.shape[1]
    G = PEER_PICKS // SC_ROWS
    R, L = SC_ROWS, SC_LANES
    tpw = N // SC_WORKERS
    nblk = tpw // SC_TOK_BLOCK
    items = SC_TOK_BLOCK * G
    nch = D // L

    @functools.partial(
        pl.kernel, mesh=_sc_mesh(), compiler_params=pltpu.CompilerParams(needs_layout_passes=False),
        out_type=jax.ShapeDtypeStruct((N, D), f32),
        scratch_types=[
            pltpu.VMEM((items, R), jnp.int32),
            pltpu.VMEM((items, R), f32),
            pltpu.VMEM((SC_TOK_BLOCK, D), f32),
            pltpu.VMEM((R, D), f32),
            pltpu.VMEM((R, D), f32),
            pltpu.SemaphoreType.DMA,
            pltpu.SemaphoreType.DMA,
        ],
        name="peer_sc_wsum",
    )
    def k(v_hbm, idx_hbm, w_hbm, o_hbm, idx_v, w_v, o_v, rows0, rows1, sem0, sem1):
        wid = lax.axis_index("s") * 2 + lax.axis_index("c")
        rows = (rows0, rows1)
        sems = (sem0, sem1)
        lane = lax.iota(jnp.int32, L)

        def gather(item, b):
            return pltpu.make_async_copy(v_hbm.at[idx_v.at[item]], rows[b], sems[b])

        def compute(item, b):
            tl = item // G
            first = (item % G) == 0
            rv = rows[b]
            for half in range(R // L):
                wv = w_v[item, pl.ds(half * L, L)]
                ws = [jnp.sum(jnp.where(lane == r, wv, 0.0)) for r in range(L)]

                def chunk(c, carry):
                    off = pl.multiple_of(c * L, L)
                    acc = rv[half * L, pl.ds(off, L)] * ws[0]
                    for r in range(1, L):
                        acc = acc + rv[half * L + r, pl.ds(off, L)] * ws[r]
                    prev = o_v[tl, pl.ds(off, L)]
                    if half == 0:
                        prev = jnp.where(first, 0.0, prev)
                    o_v[tl, pl.ds(off, L)] = acc + prev
                    return carry

                lax.fori_loop(0, nch, chunk, 0)

        @pl.loop(0, nblk)
        def _(blk):
            tok0 = wid * tpw + blk * SC_TOK_BLOCK
            it0 = tok0 * G
            pltpu.sync_copy(idx_hbm.at[pl.ds(it0, items)], idx_v)
            pltpu.sync_copy(w_hbm.at[pl.ds(it0, items)], w_v)
            gather(0, 0).start()

            @pl.loop(0, items, step=2)
            def _(it):
                gather(it + 1, 1).start()
                gather(it, 0).wait()
                compute(it, 0)

                @pl.when(it + 2 < items)
                def _():
                    gather(it + 2, 0).start()

                gather(it + 1, 1).wait()
                compute(it + 1, 1)

            pltpu.sync_copy(o_v, o_hbm.at[pl.ds(tok0, SC_TOK_BLOCK)])

    return k(v, idx, w)


def peer_ffn(x, gain, w_q, keys1, keys2, u, v):
    B, T, D = x.shape
    N = B * T
    PH, K = PEER_HEADS, PEER_TOPK
    G = PEER_PICKS // SC_ROWS
    xn, s1, s2 = _peer_scores(x.reshape(N, D), gain, w_q, keys1, keys2)
    v1, i1 = lax.top_k(s1.reshape(N, PH, PEER_NKEYS), K)
    v2, i2 = lax.top_k(s2.reshape(N, PH, PEER_NKEYS), K)
    cand = (v1[..., :, None] + v2[..., None, :]).reshape(N, PH, K * K)
    cid = (i1[..., :, None] * PEER_NKEYS + i2[..., None, :]).reshape(N, PH, K * K)
    sc, j = lax.top_k(cand, K)
    eid = jnp.take_along_axis(cid, j, axis=-1).reshape(N * G, SC_ROWS)
    gate = jax.nn.softmax(sc, axis=-1).reshape(N * G, SC_ROWS)
    h = _sc_gather_dot(u, eid, xn)
    w = gate * jax.nn.gelu(h, approximate=False)
    return _sc_gather_wsum(v, eid, w, N).reshape(B, T, D)


def kernel(x,
           l0_attn_norm, l0_ret_w_in, l0_ret_out_gain, l0_ret_w_out,
           l0_ffn_norm, l0_peer_w_q, l0_peer_keys1, l0_peer_keys2, l0_peer_u, l0_peer_v,
           l1_attn_norm, l1_gdn_w_in, l1_gdn_conv_w, l1_gdn_a_log, l1_gdn_dt_bias, l1_gdn_out_gain, l1_gdn_w_out,
           l1_ffn_norm, l1_peer_w_q, l1_peer_keys1, l1_peer_keys2, l1_peer_u, l1_peer_v,
           l2_attn_norm, l2_nsa_w_in, l2_nsa_q_gain, l2_nsa_kc_gain, l2_nsa_ks_gain, l2_nsa_kw_gain,
           l2_nsa_pe_k, l2_nsa_w1_k, l2_nsa_w2_k, l2_nsa_pe_v, l2_nsa_w1_v, l2_nsa_w2_v, l2_nsa_w_out,
           l2_ffn_norm, l2_peer_w_q, l2_peer_keys1, l2_peer_keys2, l2_peer_u, l2_peer_v,
           l3_attn_norm, l3_ret_w_in, l3_ret_out_gain, l3_ret_w_out,
           l3_ffn_norm, l3_peer_w_q, l3_peer_keys1, l3_peer_keys2, l3_peer_u, l3_peer_v):
    mixers = (retention_mixer, gated_deltanet_mixer, nsa_mixer)
    attn_norms = (l0_attn_norm, l1_attn_norm, l2_attn_norm, l3_attn_norm)
    mixer_args = (
        (l0_ret_w_in, l0_ret_out_gain, l0_ret_w_out),
        (l1_gdn_w_in, l1_gdn_conv_w, l1_gdn_a_log, l1_gdn_dt_bias, l1_gdn_out_gain, l1_gdn_w_out),
        (l2_nsa_w_in, l2_nsa_q_gain, l2_nsa_kc_gain, l2_nsa_ks_gain, l2_nsa_kw_gain,
         l2_nsa_pe_k, l2_nsa_w1_k, l2_nsa_w2_k, l2_nsa_pe_v, l2_nsa_w1_v, l2_nsa_w2_v, l2_nsa_w_out),
        (l3_ret_w_in, l3_ret_out_gain, l3_ret_w_out),
    )
    ffn_norms = (l0_ffn_norm, l1_ffn_norm, l2_ffn_norm, l3_ffn_norm)
    peer_args = (
        (l0_peer_w_q, l0_peer_keys1, l0_peer_keys2, l0_peer_u, l0_peer_v),
        (l1_peer_w_q, l1_peer_keys1, l1_peer_keys2, l1_peer_u, l1_peer_v),
        (l2_peer_w_q, l2_peer_keys1, l2_peer_keys2, l2_peer_u, l2_peer_v),
        (l3_peer_w_q, l3_peer_keys1, l3_peer_keys2, l3_peer_u, l3_peer_v),
    )
    for i in range(DEPTH):
        x = x + mixers[i % MIXER_CYCLE](rms_norm(x, attn_norms[i]), *mixer_args[i])
        x = x + peer_ffn(x, ffn_norms[i], *peer_args[i])
    return x
```

```python
import functools
import math

import jax
import jax.numpy as jnp
from jax import lax
from jax.experimental import pallas as pl
from jax.experimental.pallas import tpu as pltpu
from jax.experimental.pallas import tpu_sc as plsc

D_MODEL = 1024
DEPTH = 4
f32 = jnp.float32
bf16 = jnp.bfloat16
NORM_EPS = 1e-6
MIXER_CYCLE = 3

RET_HEADS = 8
RET_DK = D_MODEL // RET_HEADS
RET_DV = D_MODEL // RET_HEADS
RET_CHUNK = 128
RET_THETA = 10000.0

GDN_HEADS = 8
GDN_DK = D_MODEL // GDN_HEADS
GDN_DV = D_MODEL // GDN_HEADS
GDN_CONV = 4
GDN_CHUNK = 64

NSA_HEADS = 16
NSA_GROUPS = 4
NSA_HPG = NSA_HEADS // NSA_GROUPS
NSA_DH = D_MODEL // NSA_HEADS
NSA_CMP_BLOCK = 32
NSA_CMP_STRIDE = 16
NSA_CMP_HIDDEN = 256
NSA_SEL_BLOCK = 64
NSA_TOPN = 16
NSA_WINDOW = 512
NSA_QBLOCK = 32
NSA_FORCE = 1e4
ROPE_THETA = 500000.0
ROPE_DIMS = NSA_DH // 4

PEER_NKEYS = 128
PEER_EXPERTS = PEER_NKEYS * PEER_NKEYS
PEER_HEADS = 8
PEER_QDIM = 256
PEER_TOPK = 16
PEER_TOKEN_BLOCK = 128


def _mm_kernel(a_ref, b_ref, o_ref):
    o_ref[...] = jnp.dot(a_ref[...].astype(bf16), b_ref[...].astype(bf16),
                         preferred_element_type=f32)


def _mm(a, b, tm=512, tn=1024):
    M, K = a.shape
    _, N = b.shape
    tn = min(tn, N)
    if N % tn:
        tn = N
    return pl.pallas_call(
        _mm_kernel,
        grid=(M // tm, N // tn),
        in_specs=[pl.BlockSpec((tm, K), lambda i, j: (i, 0)),
                  pl.BlockSpec((K, tn), lambda i, j: (0, j))],
        out_specs=pl.BlockSpec((tm, tn), lambda i, j: (i, j)),
        out_shape=jax.ShapeDtypeStruct((M, N), f32),
        compiler_params=pltpu.CompilerParams(
            dimension_semantics=("parallel", "parallel"),
            vmem_limit_bytes=48 * 1024 * 1024),
    )(a, b)


def _proj(h, w):
    B, T, D = h.shape
    return _mm(h.reshape(B * T, D), w).reshape(B, T, -1)


def rms_norm(x, gain):
    xf = x.astype(f32)
    y = xf * lax.rsqrt(jnp.mean(xf * xf, axis=-1, keepdims=True) + NORM_EPS)
    return (y * gain.astype(f32)).astype(x.dtype)


def l2_norm(x):
    xf = x.astype(f32)
    return xf * lax.rsqrt(jnp.sum(xf * xf, axis=-1, keepdims=True) + NORM_EPS)


def rotary(x, pos, rot_dims, theta):
    half = rot_dims // 2
    inv_freq = jnp.power(theta, -jnp.arange(half, dtype=f32) / half)
    ang = pos.astype(f32)[:, None] * inv_freq[None, :]
    cos, sin = jnp.cos(ang), jnp.sin(ang)
    xf = x.astype(f32)
    x1, x2 = xf[..., :half], xf[..., half:rot_dims]
    out = jnp.concatenate([x1 * cos - x2 * sin, x2 * cos + x1 * sin, xf[..., rot_dims:]], axis=-1)
    return out.astype(x.dtype)


def masked_softmax(s, mask):
    s = jnp.where(mask, s.astype(f32), -1e30)
    m = jnp.max(s, axis=-1, keepdims=True)
    e = jnp.where(mask, jnp.exp(s - m), 0.0)
    return e / jnp.maximum(jnp.sum(e, axis=-1, keepdims=True), 1e-30)


def to_heads(z, n_heads):
    B, T, _ = z.shape
    return z.reshape(B, T, n_heads, -1).transpose(0, 2, 1, 3)


def causal_depthwise_conv(x, w):
    K, C = w.shape
    return lax.conv_general_dilated(x, w.astype(x.dtype)[:, None, :], window_strides=(1,),
                                    padding=[(K - 1, 0)], dimension_numbers=('NWC', 'WIO', 'NWC'),
                                    feature_group_count=C)


def retention_mixer(h, w_in, out_gain, w_out):
    B, T, D = h.shape
    H, dk, dv, C = RET_HEADS, RET_DK, RET_DV, RET_CHUNK
    n_chunks = T // C
    q, k, v, g = jnp.split(_proj(h, w_in), 4, axis=-1)
    pos = jnp.arange(T)
    q = rotary(to_heads(q, H), pos, dk, RET_THETA)
    k = rotary(to_heads(k, H), pos, dk, RET_THETA) * (dk ** -0.5)
    v = to_heads(v, H)

    def chunks(z):
        return z.astype(f32).reshape(B, H, n_chunks, C, -1).transpose(2, 0, 1, 3, 4)

    log_gamma = jnp.log1p(-jnp.power(2.0, -5.0 - jnp.arange(H, dtype=f32)))
    idx = jnp.arange(C, dtype=f32)
    rel = idx[:, None] - idx[None, :]
    intra = jnp.where(rel >= 0, jnp.exp(log_gamma[:, None, None] * jnp.maximum(rel, 0.0)), 0.0)
    q_decay = jnp.exp(log_gamma[:, None] * (idx + 1.0))[:, :, None]
    k_decay = jnp.exp(log_gamma[:, None] * (C - 1.0 - idx))[:, :, None]
    chunk_decay = jnp.exp(log_gamma * C)[:, None, None]

    def step(state, qkv):
        qc, kc, vc = qkv
        s = jnp.einsum('bhid,bhjd->bhij', qc, kc) * intra
        o = (jnp.einsum('bhij,bhjv->bhiv', s, vc)
             + jnp.einsum('bhid,bhdv->bhiv', qc, state) * q_decay)
        state = state * chunk_decay + jnp.einsum('bhjd,bhjv->bhdv', kc * k_decay, vc)
        return state, o

    state0 = jnp.zeros((B, H, dk, dv), f32)
    _, o = lax.scan(step, state0, (chunks(q), chunks(k), chunks(v)))
    o = o.transpose(1, 0, 3, 2, 4).reshape(B, T, H, dv)
    o = rms_norm(o, out_gain.reshape(H, dv)).reshape(B, T, H * dv).astype(h.dtype)
    return _proj(jax.nn.silu(g) * o, w_out)


def gated_deltanet_mixer(h, w_in, conv_w, a_log, dt_bias, out_gain, w_out):
    B, T, D = h.shape
    H, dk, dv, C = GDN_HEADS, GDN_DK, GDN_DV, GDN_CHUNK
    n_qk, n_v = H * dk, H * dv
    n_conv = 2 * n_qk + n_v
    n = T // C
    proj = jnp.dot(h, w_in)
    qkv = jax.nn.silu(causal_depthwise_conv(proj[..., :n_conv], conv_w))
    z = proj[..., n_conv:n_conv + n_v]
    beta = jax.nn.sigmoid(proj[..., n_conv + n_v:n_conv + n_v + H].astype(f32))
    a = proj[..., n_conv + n_v + H:].astype(f32)
    g = -jnp.exp(a_log.astype(f32)) * jax.nn.softplus(a + dt_bias.astype(f32))

    def chunks(zz):
        return zz.reshape(B, H, n, C, -1)

    q = chunks(l2_norm(to_heads(qkv[..., :n_qk], H)) * (dk ** -0.5))
    k = chunks(l2_norm(to_heads(qkv[..., n_qk:2 * n_qk], H)))
    v = chunks(to_heads(qkv[..., 2 * n_qk:], H).astype(f32))
    beta = beta.transpose(0, 2, 1).reshape(B, H, n, C)
    g_cum = jnp.cumsum(g.transpose(0, 2, 1).reshape(B, H, n, C), axis=-1)

    tril = jnp.tril(jnp.ones((C, C), bool))
    strict = jnp.tril(jnp.ones((C, C), f32), -1)
    diff = g_cum[..., :, None] - g_cum[..., None, :]
    decay = jnp.where(tril, jnp.exp(jnp.minimum(diff, 0.0)), 0.0)
    k_beta = k * beta[..., None]
    lower = jnp.einsum('bhnid,bhnjd->bhnij', k_beta, k) * decay * strict
    eye = jnp.eye(C, dtype=f32)
    t_mat = lax.linalg.triangular_solve(eye + lower, jnp.broadcast_to(eye, lower.shape),
                                        left_side=True, lower=True)
    u = jnp.einsum('bhnij,bhnjv->bhniv', t_mat, v * beta[..., None])
    w = jnp.einsum('bhnij,bhnjd->bhnid', t_mat, k_beta * jnp.exp(g_cum)[..., None])
    qk = jnp.einsum('bhnid,bhnjd->bhnij', q, k) * decay
    q_dec = q * jnp.exp(g_cum)[..., None]
    g_last = g_cum[..., -1]
    k_dec = k * jnp.exp(g_last[..., None] - g_cum)[..., None]

    def mv(zz):
        return jnp.moveaxis(zz, 2, 0)

    def step(S, inp):
        qk_i, q_dec_i, k_dec_i, u_i, w_i, g_last_i = inp
        v_new = u_i - jnp.einsum('bhcd,bhdv->bhcv', w_i, S)
        o = jnp.einsum('bhcd,bhdv->bhcv', q_dec_i, S) + jnp.einsum('bhij,bhjv->bhiv', qk_i, v_new)
        S = S * jnp.exp(g_last_i)[..., None, None] + jnp.einsum('bhcd,bhcv->bhdv', k_dec_i, v_new)
        return S, o

    S0 = jnp.zeros((B, H, dk, dv), f32)
    _, o = lax.scan(step, S0, (mv(qk), mv(q_dec), mv(k_dec), mv(u), mv(w), jnp.moveaxis(g_last, 2, 0)))
    o = o.transpose(1, 0, 3, 2, 4).reshape(B, T, H, dv)
    o = rms_norm(o, out_gain) * jax.nn.silu(z.reshape(B, T, H, dv).astype(f32))
    return _proj(o.reshape(B, T, H * dv).astype(h.dtype), w_out)


def compress_blocks(zz, idx, pe, w1, w2):
    blocks = zz[:, :, idx] + pe
    flat = blocks.reshape(blocks.shape[0], blocks.shape[1], blocks.shape[2], -1)
    return jax.nn.gelu(flat @ w1, approximate=False) @ w2


def nsa_mixer(h, w_in, q_gain, kc_gain, ks_gain, kw_gain, pe_k, w1_k, w2_k, pe_v, w1_v, w2_v, w_out):
    B, T, D = h.shape
    H, G, Hg, dh = NSA_HEADS, NSA_GROUPS, NSA_HPG, NSA_DH
    kvw = G * dh
    L, S_, SEL, W, QB = NSA_CMP_BLOCK, NSA_CMP_STRIDE, NSA_SEL_BLOCK, NSA_WINDOW, NSA_QBLOCK
    n_cmp = (T - L) // S_ + 1
    n_sel = T // SEL
    top_n = min(NSA_TOPN, n_sel)

    proj = jnp.dot(h, w_in)
    q = proj[..., :H * dh].reshape(B, T, H, dh)
    kc, vc, ks, vs, kw, vw = [proj[..., H * dh + j * kvw:H * dh + (j + 1) * kvw]
                              .reshape(B, T, G, dh).transpose(0, 2, 1, 3) for j in range(6)]
    gates = jax.nn.sigmoid(proj[..., H * dh + 6 * kvw:].astype(f32))
    gates = gates.reshape(B, T, 3, H).transpose(0, 2, 3, 1).reshape(B, 3, G, Hg, T)

    pos = jnp.arange(T)
    q = rotary(rms_norm(q, q_gain).transpose(0, 2, 1, 3), pos, ROPE_DIMS, ROPE_THETA)
    q = q.reshape(B, G, Hg, T, dh) * (dh ** -0.5)
    ks = rotary(rms_norm(ks, ks_gain), pos, ROPE_DIMS, ROPE_THETA)
    kw = rotary(rms_norm(kw, kw_gain), pos, ROPE_DIMS, ROPE_THETA)

    cmp_start = jnp.arange(n_cmp) * S_
    cmp_end = cmp_start + L - 1
    cmp_idx = cmp_start[:, None] + jnp.arange(L)[None, :]
    k_cmp = rotary(rms_norm(compress_blocks(kc, cmp_idx, pe_k, w1_k, w2_k), kc_gain), cmp_end,
                   ROPE_DIMS, ROPE_THETA)
    v_cmp = compress_blocks(vc, cmp_idx, pe_v, w1_v, w2_v)

    sel_start = jnp.arange(n_sel) * SEL
    overlap = jnp.clip(jnp.minimum(cmp_start[:, None] + L, sel_start[None, :] + SEL)
                       - jnp.maximum(cmp_start[:, None], sel_start[None, :]), 0).astype(f32) / L
    ks_blk = ks.reshape(B, G, n_sel, SEL, dh)
    vs_blk = vs.reshape(B, G, n_sel, SEL, dh)
    kw_pad = jnp.pad(kw, ((0, 0), (0, 0), (W, 0), (0, 0)))
    vw_pad = jnp.pad(vw, ((0, 0), (0, 0), (W, 0), (0, 0)))
    gather_blocks = jax.vmap(jax.vmap(lambda blocks, ix: blocks[ix]))
    blk_ids = jnp.arange(n_sel)

    def query_block(i):
        start = i * QB
        tq = start + jnp.arange(QB)
        qi = lax.dynamic_slice_in_dim(q, start, QB, axis=3)
        gi = lax.dynamic_slice_in_dim(gates, start, QB, axis=4)
        s = jnp.einsum('bghqd,bgcd->bghqc', qi, k_cmp)
        p_cmp = masked_softmax(s, cmp_end[None, :] <= tq[:, None])
        o_cmp = jnp.einsum('bghqc,bgcd->bghqd', p_cmp, v_cmp)
        imp = jnp.einsum('bghqc,cj->bgqj', p_cmp, overlap)
        cur = tq // SEL
        forced = ((blk_ids[None, :] == 0) | (blk_ids[None, :] == cur[:, None])
                  | (blk_ids[None, :] == cur[:, None] - 1))
        valid = blk_ids[None, :] <= cur[:, None]
        score = jnp.where(forced, NSA_FORCE, jnp.where(valid, imp, -NSA_FORCE))
        _, sel = lax.top_k(score, top_n)
        k_sel = gather_blocks(ks_blk, sel).reshape(B, G, QB, top_n * SEL, dh)
        v_sel = gather_blocks(vs_blk, sel).reshape(B, G, QB, top_n * SEL, dh)
        kpos = (sel[..., None] * SEL + jnp.arange(SEL)).reshape(B, G, QB, top_n * SEL)
        s = jnp.einsum('bghqd,bgqkd->bghqk', qi, k_sel)
        p = masked_softmax(s, (kpos <= tq[None, None, :, None])[:, :, None])
        o_slc = jnp.einsum('bghqk,bgqkd->bghqd', p, v_sel)
        kwin = lax.dynamic_slice_in_dim(kw_pad, start, W + QB, axis=2)
        vwin = lax.dynamic_slice_in_dim(vw_pad, start, W + QB, axis=2)
        wpos = start - W + jnp.arange(W + QB)
        dpos = tq[:, None] - wpos[None, :]
        s = jnp.einsum('bghqd,bgkd->bghqk', qi, kwin)
        p = masked_softmax(s, (dpos >= 0) & (dpos < W) & (wpos[None, :] >= 0))
        o_win = jnp.einsum('bghqk,bgkd->bghqd', p, vwin)
        o = (gi[:, 0][..., None] * o_cmp + gi[:, 1][..., None] * o_slc + gi[:, 2][..., None] * o_win)
        return o.astype(h.dtype)

    o = lax.map(query_block, jnp.arange(T // QB))
    o = o.transpose(1, 0, 4, 2, 3, 5).reshape(B, T, H * dh)
    return _proj(o, w_out)


PEER_TM = 128
SC_LANES = 16
SC_WORKERS = 32
SC_ROWS = 32
SC_TOK_BLOCK = 8
PEER_PICKS = PEER_HEADS * PEER_TOPK


def _top_rows(s, k, payload=None):
    R = s.shape[0]
    rows = lax.broadcasted_iota(jnp.int32, s.shape, 0)
    vals, ids = [], []
    for _ in range(k):
        m = jnp.max(s, axis=0, keepdims=True)
        first = jnp.min(jnp.where(s == m, rows, R), axis=0, keepdims=True)
        hit = rows == first
        vals.append(m)
        if payload is None:
            ids.append(first)
        else:
            ids.append(jnp.max(jnp.where(hit, payload, -1), axis=0, keepdims=True))
        s = jnp.where(hit, -jnp.inf, s)
    return jnp.concatenate(vals, axis=0), jnp.concatenate(ids, axis=0)


def _peer_route_kernel(x_ref, g_ref, wq_ref, k1_ref, k2_ref, xn_ref, eid_ref, gate_ref, eid_t, gate_t):
    x = x_ref[...]
    xn = x * lax.rsqrt(jnp.mean(x * x, axis=-1, keepdims=True) + NORM_EPS) * g_ref[...]
    xn_ref[...] = xn
    q = jnp.dot(xn.astype(bf16), wq_ref[...], preferred_element_type=f32)
    half = PEER_QDIM // 2
    K = PEER_TOPK
    k1 = k1_ref[...]
    k2 = k2_ref[...]
    dn = (((1,), (1,)), ((), ()))
    for h in range(PEER_HEADS):
        q1 = q[:, h * PEER_QDIM:h * PEER_QDIM + half].astype(bf16)
        q2 = q[:, h * PEER_QDIM + half:(h + 1) * PEER_QDIM].astype(bf16)
        s1 = lax.dot_general(k1, q1, dn, preferred_element_type=f32)
        s2 = lax.dot_general(k2, q2, dn, preferred_element_type=f32)
        v1, i1 = _top_rows(s1, K)
        v2, i2 = _top_rows(s2, K)
        cand = jnp.concatenate([v1[i:i + 1, :] + v2 for i in range(K)], axis=0)
        cid = jnp.concatenate([i1[i:i + 1, :] * PEER_NKEYS + i2 for i in range(K)], axis=0)
        sc, eid = _top_rows(cand, K, payload=cid)
        e = jnp.exp(sc - jnp.max(sc, axis=0, keepdims=True))
        eid_t[h * K:(h + 1) * K, :] = eid
        gate_t[h * K:(h + 1) * K, :] = e / jnp.sum(e, axis=0, keepdims=True)
    eid_ref[...] = eid_t[...].T
    gate_ref[...] = gate_t[...].T


def _peer_route(x2, gain, w_q, keys1, keys2):
    N, D = x2.shape
    nq = PEER_HEADS * PEER_QDIM
    P = PEER_PICKS
    tm = PEER_TM
    return pl.pallas_call(
        _peer_route_kernel,
        grid=(N // tm,),
        in_specs=[pl.BlockSpec((tm, D), lambda i: (i, 0)),
                  pl.BlockSpec((1, D), lambda i: (0, 0)),
                  pl.BlockSpec((D, nq), lambda i: (0, 0)),
                  pl.BlockSpec(keys1.shape, lambda i: (0, 0)),
                  pl.BlockSpec(keys2.shape, lambda i: (0, 0))],
        out_specs=[pl.BlockSpec((tm, D), lambda i: (i, 0)),
                   pl.BlockSpec((tm, P), lambda i: (i, 0)),
                   pl.BlockSpec((tm, P), lambda i: (i, 0))],
        out_shape=[jax.ShapeDtypeStruct((N, D), f32),
                   jax.ShapeDtypeStruct((N, P), jnp.int32),
                   jax.ShapeDtypeStruct((N, P), f32)],
        scratch_shapes=[pltpu.VMEM((P, tm), jnp.int32), pltpu.VMEM((P, tm), f32)],
        compiler_params=pltpu.CompilerParams(
            dimension_semantics=("parallel",), vmem_limit_bytes=48 * 1024 * 1024),
        name="peer_route",
    )(x2, gain.reshape(1, D), w_q.astype(bf16), keys1.astype(bf16), keys2.astype(bf16))


def _sc_mesh():
    return plsc.VectorSubcoreMesh(core_axis_name="c", subcore_axis_name="s")


def _sc_gather_dot(u, idx, x2):
    N, D = x2.shape
    G = PEER_PICKS // SC_ROWS
    R, L = SC_ROWS, SC_LANES
    tpw = N // SC_WORKERS
    nblk = tpw // SC_TOK_BLOCK
    items = SC_TOK_BLOCK * G
    nch = D // L

    @functools.partial(
        pl.kernel, mesh=_sc_mesh(), compiler_params=pltpu.CompilerParams(needs_layout_passes=False),
        out_type=jax.ShapeDtypeStruct((N * G, R), f32),
        scratch_types=[
            pltpu.VMEM((items, R), jnp.int32),
            pltpu.VMEM((SC_TOK_BLOCK, D), f32),
            pltpu.VMEM((items, R), f32),
            pltpu.VMEM((R, D), f32),
            pltpu.VMEM((R, D), f32),
            pltpu.SemaphoreType.DMA,
            pltpu.SemaphoreType.DMA,
        ],
        name="peer_sc_dot",
    )
    def k(u_hbm, idx_hbm, x_hbm, h_hbm, idx_v, x_v, h_v, rows0, rows1, sem0, sem1):
        wid = lax.axis_index("s") * 2 + lax.axis_index("c")
        rows = (rows0, rows1)
        sems = (sem0, sem1)
        lane = lax.iota(jnp.int32, L)

        def gather(item, b):
            return pltpu.make_async_copy(u_hbm.at[idx_v.at[item]], rows[b], sems[b])

        def compute(item, b):
            tl = item // G
            rv = rows[b]

            def chunk(c, accs):
                off = pl.multiple_of(c * L, L)
                xc = x_v[tl, pl.ds(off, L)]
                return tuple(accs[r] + rv[r, pl.ds(off, L)] * xc for r in range(R))

            accs = lax.fori_loop(0, nch, chunk, tuple(jnp.zeros((L,), f32) for _ in range(R)))
            for half in range(R // L):
                out = jnp.zeros((L,), f32)
                for r in range(L):
                    out = jnp.where(lane == r, jnp.sum(accs[half * L + r]), out)
                h_v[item, pl.ds(half * L, L)] = out

        @pl.loop(0, nblk)
        def _(blk):
            tok0 = wid * tpw + blk * SC_TOK_BLOCK
            it0 = tok0 * G
            pltpu.sync_copy(idx_hbm.at[pl.ds(it0, items)], idx_v)
            pltpu.sync_copy(x_hbm.at[pl.ds(tok0, SC_TOK_BLOCK)], x_v)
            gather(0, 0).start()

            @pl.loop(0, items, step=2)
            def _(it):
                gather(it + 1, 1).start()
                gather(it, 0).wait()
                compute(it, 0)

                @pl.when(it + 2 < items)
                def _():
                    gather(it + 2, 0).start()

                gather(it + 1, 1).wait()
                compute(it + 1, 1)

            pltpu.sync_copy(h_v, h_hbm.at[pl.ds(it0, items)])

    return k(u, idx, x2)


def _sc_gather_wsum(v, idx, w, N):
    D = v.shape[1]
    G = PEER_PICKS // SC_ROWS
    R, L = SC_ROWS, SC_LANES
    tpw = N // SC_WORKERS
    nblk = tpw // SC_TOK_BLOCK
    items = SC_TOK_BLOCK * G
    nch = D // L

    @functools.partial(
        pl.kernel, mesh=_sc_mesh(), compiler_params=pltpu.CompilerParams(needs_layout_passes=False),
        out_type=jax.ShapeDtypeStruct((N, D), f32),
        scratch_types=[
            pltpu.VMEM((items, R), jnp.int32),
            pltpu.VMEM((items, R), f32),
            pltpu.VMEM((SC_TOK_BLOCK, D), f32),
            pltpu.VMEM((R, D), f32),
            pltpu.VMEM((R, D), f32),
            pltpu.SemaphoreType.DMA,
            pltpu.SemaphoreType.DMA,
        ],
        name="peer_sc_wsum",
    )
    def k(v_hbm, idx_hbm, w_hbm, o_hbm, idx_v, w_v, o_v, rows0, rows1, sem0, sem1):
        wid = lax.axis_index("s") * 2 + lax.axis_index("c")
        rows = (rows0, rows1)
        sems = (sem0, sem1)
        lane = lax.iota(jnp.int32, L)

        def gather(item, b):
            return pltpu.make_async_copy(v_hbm.at[idx_v.at[item]], rows[b], sems[b])

        def compute(item, b):
            tl = item // G
            first = (item % G) == 0
            rv = rows[b]
            for half in range(R // L):
                wv = w_v[item, pl.ds(half * L, L)]
                ws = [jnp.sum(jnp.where(lane == r, wv, 0.0)) for r in range(L)]

                @plsc.parallel_loop(0, nch, unroll=2)
                def _(c):
                    off = pl.multiple_of(c * L, L)
                    prods = [rv[half * L + r, pl.ds(off, L)] * ws[r] for r in range(L)]
                    while len(prods) > 1:
                        prods = [a + b for a, b in zip(prods[::2], prods[1::2])]
                    prev = o_v[tl, pl.ds(off, L)]
                    if half == 0:
                        prev = jnp.where(first, 0.0, prev)
                    o_v[tl, pl.ds(off, L)] = prods[0] + prev

        @pl.loop(0, nblk)
        def _(blk):
            tok0 = wid * tpw + blk * SC_TOK_BLOCK
            it0 = tok0 * G
            pltpu.sync_copy(idx_hbm.at[pl.ds(it0, items)], idx_v)
            pltpu.sync_copy(w_hbm.at[pl.ds(it0, items)], w_v)
            gather(0, 0).start()

            @pl.loop(0, items, step=2)
            def _(it):
                gather(it + 1, 1).start()
                gather(it, 0).wait()
                compute(it, 0)

                @pl.when(it + 2 < items)
                def _():
                    gather(it + 2, 0).start()

                gather(it + 1, 1).wait()
                compute(it + 1, 1)

            pltpu.sync_copy(o_v, o_hbm.at[pl.ds(tok0, SC_TOK_BLOCK)])

    return k(v, idx, w)


def peer_ffn(x, gain, w_q, keys1, keys2, u, v):
    B, T, D = x.shape
    N = B * T
    G = PEER_PICKS // SC_ROWS
    xn, eid, gate = _peer_route(x.reshape(N, D), gain, w_q, keys1, keys2)
    eid = eid.reshape(N * G, SC_ROWS)
    gate = gate.reshape(N * G, SC_ROWS)
    h = _sc_gather_dot(u, eid, xn)
    w = gate * jax.nn.gelu(h, approximate=False)
    return _sc_gather_wsum(v, eid, w, N).reshape(B, T, D)


def kernel(x,
           l0_attn_norm, l0_ret_w_in, l0_ret_out_gain, l0_ret_w_out,
           l0_ffn_norm, l0_peer_w_q, l0_peer_keys1, l0_peer_keys2, l0_peer_u, l0_peer_v,
           l1_attn_norm, l1_gdn_w_in, l1_gdn_conv_w, l1_gdn_a_log, l1_gdn_dt_bias, l1_gdn_out_gain, l1_gdn_w_out,
           l1_ffn_norm, l1_peer_w_q, l1_peer_keys1, l1_peer_keys2, l1_peer_u, l1_peer_v,
           l2_attn_norm, l2_nsa_w_in, l2_nsa_q_gain, l2_nsa_kc_gain, l2_nsa_ks_gain, l2_nsa_kw_gain,
           l2_nsa_pe_k, l2_nsa_w1_k, l2_nsa_w2_k, l2_nsa_pe_v, l2_nsa_w1_v, l2_nsa_w2_v, l2_nsa_w_out,
           l2_ffn_norm, l2_peer_w_q, l2_peer_keys1, l2_peer_keys2, l2_peer_u, l2_peer_v,
           l3_attn_norm, l3_ret_w_in, l3_ret_out_gain, l3_ret_w_out,
           l3_ffn_norm, l3_peer_w_q, l3_peer_keys1, l3_peer_keys2, l3_peer_u, l3_peer_v):
    mixers = (retention_mixer, gated_deltanet_mixer, nsa_mixer)
    attn_norms = (l0_attn_norm, l1_attn_norm, l2_attn_norm, l3_attn_norm)
    mixer_args = (
        (l0_ret_w_in, l0_ret_out_gain, l0_ret_w_out),
        (l1_gdn_w_in, l1_gdn_conv_w, l1_gdn_a_log, l1_gdn_dt_bias, l1_gdn_out_gain, l1_gdn_w_out),
        (l2_nsa_w_in, l2_nsa_q_gain, l2_nsa_kc_gain, l2_nsa_ks_gain, l2_nsa_kw_gain,
         l2_nsa_pe_k, l2_nsa_w1_k, l2_nsa_w2_k, l2_nsa_pe_v, l2_nsa_w1_v, l2_nsa_w2_v, l2_nsa_w_out),
        (l3_ret_w_in, l3_ret_out_gain, l3_ret_w_out),
    )
    ffn_norms = (l0_ffn_norm, l1_ffn_norm, l2_ffn_norm, l3_ffn_norm)
    peer_args = (
        (l0_peer_w_q, l0_peer_keys1, l0_peer_keys2, l0_peer_u, l0_peer_v),
        (l1_peer_w_q, l1_peer_keys1, l1_peer_keys2, l1_peer_u, l1_peer_v),
        (l2_peer_w_q, l2_peer_keys1, l2_peer_keys2, l2_peer_u, l2_peer_v),
        (l3_peer_w_q, l3_peer_keys1, l3_peer_keys2, l3_peer_u, l3_peer_v),
    )
    for i in range(DEPTH):
        x = x + mixers[i % MIXER_CYCLE](rms_norm(x, attn_norms[i]), *mixer_args[i])
        x = x + peer_ffn(x, ffn_norms[i], *peer_args[i])
    return x
```

```python
import functools
import math

import jax
import jax.numpy as jnp
from jax import lax
from jax.experimental import pallas as pl
from jax.experimental.pallas import tpu as pltpu
from jax.experimental.pallas import tpu_sc as plsc

D_MODEL = 1024
DEPTH = 4
f32 = jnp.float32
bf16 = jnp.bfloat16
NORM_EPS = 1e-6
MIXER_CYCLE = 3

RET_HEADS = 8
RET_DK = D_MODEL // RET_HEADS
RET_DV = D_MODEL // RET_HEADS
RET_CHUNK = 128
RET_THETA = 10000.0

GDN_HEADS = 8
GDN_DK = D_MODEL // GDN_HEADS
GDN_DV = D_MODEL // GDN_HEADS
GDN_CONV = 4
GDN_CHUNK = 64

NSA_HEADS = 16
NSA_GROUPS = 4
NSA_HPG = NSA_HEADS // NSA_GROUPS
NSA_DH = D_MODEL // NSA_HEADS
NSA_CMP_BLOCK = 32
NSA_CMP_STRIDE = 16
NSA_CMP_HIDDEN = 256
NSA_SEL_BLOCK = 64
NSA_TOPN = 16
NSA_WINDOW = 512
NSA_QBLOCK = 32
NSA_FORCE = 1e4
ROPE_THETA = 500000.0
ROPE_DIMS = NSA_DH // 4

PEER_NKEYS = 128
PEER_EXPERTS = PEER_NKEYS * PEER_NKEYS
PEER_HEADS = 8
PEER_QDIM = 256
PEER_TOPK = 16
PEER_TOKEN_BLOCK = 128


def _mm_kernel(a_ref, b_ref, o_ref):
    o_ref[...] = jnp.dot(a_ref[...].astype(bf16), b_ref[...].astype(bf16),
                         preferred_element_type=f32)


def _mm(a, b, tm=512, tn=1024):
    M, K = a.shape
    _, N = b.shape
    tn = min(tn, N)
    if N % tn:
        tn = N
    return pl.pallas_call(
        _mm_kernel,
        grid=(M // tm, N // tn),
        in_specs=[pl.BlockSpec((tm, K), lambda i, j: (i, 0)),
                  pl.BlockSpec((K, tn), lambda i, j: (0, j))],
        out_specs=pl.BlockSpec((tm, tn), lambda i, j: (i, j)),
        out_shape=jax.ShapeDtypeStruct((M, N), f32),
        compiler_params=pltpu.CompilerParams(
            dimension_semantics=("parallel", "parallel"),
            vmem_limit_bytes=48 * 1024 * 1024),
    )(a, b)


def _proj(h, w):
    B, T, D = h.shape
    return _mm(h.reshape(B * T, D), w).reshape(B, T, -1)


def rms_norm(x, gain):
    xf = x.astype(f32)
    y = xf * lax.rsqrt(jnp.mean(xf * xf, axis=-1, keepdims=True) + NORM_EPS)
    return (y * gain.astype(f32)).astype(x.dtype)


def l2_norm(x):
    xf = x.astype(f32)
    return xf * lax.rsqrt(jnp.sum(xf * xf, axis=-1, keepdims=True) + NORM_EPS)


def rotary(x, pos, rot_dims, theta):
    half = rot_dims // 2
    inv_freq = jnp.power(theta, -jnp.arange(half, dtype=f32) / half)
    ang = pos.astype(f32)[:, None] * inv_freq[None, :]
    cos, sin = jnp.cos(ang), jnp.sin(ang)
    xf = x.astype(f32)
    x1, x2 = xf[..., :half], xf[..., half:rot_dims]
    out = jnp.concatenate([x1 * cos - x2 * sin, x2 * cos + x1 * sin, xf[..., rot_dims:]], axis=-1)
    return out.astype(x.dtype)


def masked_softmax(s, mask):
    s = jnp.where(mask, s.astype(f32), -1e30)
    m = jnp.max(s, axis=-1, keepdims=True)
    e = jnp.where(mask, jnp.exp(s - m), 0.0)
    return e / jnp.maximum(jnp.sum(e, axis=-1, keepdims=True), 1e-30)


def to_heads(z, n_heads):
    B, T, _ = z.shape
    return z.reshape(B, T, n_heads, -1).transpose(0, 2, 1, 3)


def causal_depthwise_conv(x, w):
    K, C = w.shape
    return lax.conv_general_dilated(x, w.astype(x.dtype)[:, None, :], window_strides=(1,),
                                    padding=[(K - 1, 0)], dimension_numbers=('NWC', 'WIO', 'NWC'),
                                    feature_group_count=C)


def retention_mixer(h, w_in, out_gain, w_out):
    B, T, D = h.shape
    H, dk, dv, C = RET_HEADS, RET_DK, RET_DV, RET_CHUNK
    n_chunks = T // C
    q, k, v, g = jnp.split(_proj(h, w_in), 4, axis=-1)
    pos = jnp.arange(T)
    q = rotary(to_heads(q, H), pos, dk, RET_THETA)
    k = rotary(to_heads(k, H), pos, dk, RET_THETA) * (dk ** -0.5)
    v = to_heads(v, H)

    def chunks(z):
        return z.astype(f32).reshape(B, H, n_chunks, C, -1).transpose(2, 0, 1, 3, 4)

    log_gamma = jnp.log1p(-jnp.power(2.0, -5.0 - jnp.arange(H, dtype=f32)))
    idx = jnp.arange(C, dtype=f32)
    rel = idx[:, None] - idx[None, :]
    intra = jnp.where(rel >= 0, jnp.exp(log_gamma[:, None, None] * jnp.maximum(rel, 0.0)), 0.0)
    q_decay = jnp.exp(log_gamma[:, None] * (idx + 1.0))[:, :, None]
    k_decay = jnp.exp(log_gamma[:, None] * (C - 1.0 - idx))[:, :, None]
    chunk_decay = jnp.exp(log_gamma * C)[:, None, None]

    def step(state, qkv):
        qc, kc, vc = qkv
        s = jnp.einsum('bhid,bhjd->bhij', qc, kc) * intra
        o = (jnp.einsum('bhij,bhjv->bhiv', s, vc)
             + jnp.einsum('bhid,bhdv->bhiv', qc, state) * q_decay)
        state = state * chunk_decay + jnp.einsum('bhjd,bhjv->bhdv', kc * k_decay, vc)
        return state, o

    state0 = jnp.zeros((B, H, dk, dv), f32)
    _, o = lax.scan(step, state0, (chunks(q), chunks(k), chunks(v)))
    o = o.transpose(1, 0, 3, 2, 4).reshape(B, T, H, dv)
    o = rms_norm(o, out_gain.reshape(H, dv)).reshape(B, T, H * dv).astype(h.dtype)
    return _proj(jax.nn.silu(g) * o, w_out)


def gated_deltanet_mixer(h, w_in, conv_w, a_log, dt_bias, out_gain, w_out):
    B, T, D = h.shape
    H, dk, dv, C = GDN_HEADS, GDN_DK, GDN_DV, GDN_CHUNK
    n_qk, n_v = H * dk, H * dv
    n_conv = 2 * n_qk + n_v
    n = T // C
    proj = jnp.dot(h, w_in)
    qkv = jax.nn.silu(causal_depthwise_conv(proj[..., :n_conv], conv_w))
    z = proj[..., n_conv:n_conv + n_v]
    beta = jax.nn.sigmoid(proj[..., n_conv + n_v:n_conv + n_v + H].astype(f32))
    a = proj[..., n_conv + n_v + H:].astype(f32)
    g = -jnp.exp(a_log.astype(f32)) * jax.nn.softplus(a + dt_bias.astype(f32))

    def chunks(zz):
        return zz.reshape(B, H, n, C, -1)

    q = chunks(l2_norm(to_heads(qkv[..., :n_qk], H)) * (dk ** -0.5))
    k = chunks(l2_norm(to_heads(qkv[..., n_qk:2 * n_qk], H)))
    v = chunks(to_heads(qkv[..., 2 * n_qk:], H).astype(f32))
    beta = beta.transpose(0, 2, 1).reshape(B, H, n, C)
    g_cum = jnp.cumsum(g.transpose(0, 2, 1).reshape(B, H, n, C), axis=-1)

    tril = jnp.tril(jnp.ones((C, C), bool))
    strict = jnp.tril(jnp.ones((C, C), f32), -1)
    diff = g_cum[..., :, None] - g_cum[..., None, :]
    decay = jnp.where(tril, jnp.exp(jnp.minimum(diff, 0.0)), 0.0)
    k_beta = k * beta[..., None]
    lower = jnp.einsum('bhnid,bhnjd->bhnij', k_beta, k) * decay * strict
    eye = jnp.eye(C, dtype=f32)
    t_mat = lax.linalg.triangular_solve(eye + lower, jnp.broadcast_to(eye, lower.shape),
                                        left_side=True, lower=True)
    u = jnp.einsum('bhnij,bhnjv->bhniv', t_mat, v * beta[..., None])
    w = jnp.einsum('bhnij,bhnjd->bhnid', t_mat, k_beta * jnp.exp(g_cum)[..., None])
    qk = jnp.einsum('bhnid,bhnjd->bhnij', q, k) * decay
    q_dec = q * jnp.exp(g_cum)[..., None]
    g_last = g_cum[..., -1]
    k_dec = k * jnp.exp(g_last[..., None] - g_cum)[..., None]

    def mv(zz):
        return jnp.moveaxis(zz, 2, 0)

    def step(S, inp):
        qk_i, q_dec_i, k_dec_i, u_i, w_i, g_last_i = inp
        v_new = u_i - jnp.einsum('bhcd,bhdv->bhcv', w_i, S)
        o = jnp.einsum('bhcd,bhdv->bhcv', q_dec_i, S) + jnp.einsum('bhij,bhjv->bhiv', qk_i, v_new)
        S = S * jnp.exp(g_last_i)[..., None, None] + jnp.einsum('bhcd,bhcv->bhdv', k_dec_i, v_new)
        return S, o

    S0 = jnp.zeros((B, H, dk, dv), f32)
    _, o = lax.scan(step, S0, (mv(qk), mv(q_dec), mv(k_dec), mv(u), mv(w), jnp.moveaxis(g_last, 2, 0)))
    o = o.transpose(1, 0, 3, 2, 4).reshape(B, T, H, dv)
    o = rms_norm(o, out_gain) * jax.nn.silu(z.reshape(B, T, H, dv).astype(f32))
    return _proj(o.reshape(B, T, H * dv).astype(h.dtype), w_out)


def compress_blocks(zz, idx, pe, w1, w2):
    blocks = zz[:, :, idx] + pe
    flat = blocks.reshape(blocks.shape[0], blocks.shape[1], blocks.shape[2], -1)
    return jax.nn.gelu(flat @ w1, approximate=False) @ w2


NSA_TQ = 128
NSA_TK = 512
LANE = 128
NEG = -1e30


def _nsa_kernel(q_ref, kc_ref, vc_ref, ks_ref, vs_ref, kw_ref, vw_ref, g_ref, ovl_ref, exp_ref, o_ref, *, top_n):
    Hg, TQ, dh = q_ref.shape
    TK, W = NSA_TK, NSA_WINDOW
    ncp = kc_ref.shape[0]
    nsp = exp_ref.shape[0]
    t0 = pl.program_id(2) * TQ
    q2 = q_ref[...].reshape(Hg * TQ, dh)
    tpos = t0 + lax.broadcasted_iota(jnp.int32, (TQ, 1), 0)
    dn = (((1,), (1,)), ((), ()))

    def softmax_rows(s, mask):
        s3 = jnp.where(mask[None], s.reshape(Hg, TQ, -1), NEG)
        m = jnp.max(s3, axis=-1, keepdims=True)
        e = jnp.where(mask[None], jnp.exp(s3 - m), 0.0)
        return e / jnp.maximum(jnp.sum(e, axis=-1, keepdims=True), 1e-30)

    s = lax.dot_general(q2, kc_ref[...], dn, preferred_element_type=f32)
    cend = lax.broadcasted_iota(jnp.int32, (1, ncp), 1) * NSA_CMP_STRIDE + (NSA_CMP_BLOCK - 1)
    p = softmax_rows(s, cend <= tpos)
    o_cmp = jnp.dot(p.reshape(Hg * TQ, ncp).astype(bf16), vc_ref[...], preferred_element_type=f32)

    p_all = jnp.concatenate([p[h] for h in range(Hg)], axis=-1).astype(bf16)
    imp = jnp.dot(p_all, ovl_ref[...], preferred_element_type=f32)
    blk = lax.broadcasted_iota(jnp.int32, (1, nsp), 1)
    cur = tpos // NSA_SEL_BLOCK
    forced = (blk == 0) | (blk == cur) | (blk == cur - 1)
    score = jnp.where(forced, NSA_FORCE, jnp.where(blk <= cur, imp, -NSA_FORCE))
    n_real = exp_ref.shape[1] // NSA_SEL_BLOCK
    score = jnp.where(blk < n_real, score, -jnp.inf)
    st = score.T
    rows = lax.broadcasted_iota(jnp.int32, st.shape, 0)
    sel_t = jnp.zeros(st.shape, f32)
    for _ in range(top_n):
        m = jnp.max(st, axis=0, keepdims=True)
        first = jnp.min(jnp.where(st == m, rows, nsp), axis=0, keepdims=True)
        hit = rows == first
        sel_t = jnp.where(hit, 1.0, sel_t)
        st = jnp.where(hit, -jnp.inf, st)
    sel = sel_t.T.astype(bf16)

    n_tiles = (t0 + TQ + TK - 1) // TK

    def sel_tile(kt, carry):
        m, l, acc = carry
        koff = pl.multiple_of(kt * TK, TK)
        k_t = ks_ref[pl.ds(koff, TK), :]
        v_t = vs_ref[pl.ds(koff, TK), :]
        s = lax.dot_general(q2, k_t, dn, preferred_element_type=f32)
        picked = jnp.dot(sel, exp_ref[:, pl.ds(koff, TK)], preferred_element_type=f32)
        kpos = koff + lax.broadcasted_iota(jnp.int32, (1, TK), 1)
        mask = ((picked > 0.5) & (kpos <= tpos))[None]
        s3 = jnp.where(mask, s.reshape(Hg, TQ, TK), NEG)
        m_new = jnp.maximum(m, jnp.max(s3, axis=-1, keepdims=True))
        alpha = jnp.exp(m - m_new)
        e = jnp.where(mask, jnp.exp(s3 - m_new), 0.0)
        l = l * alpha + jnp.sum(e, axis=-1, keepdims=True)
        pv = jnp.dot(e.reshape(Hg * TQ, TK).astype(bf16), v_t, preferred_element_type=f32)
        return m_new, l, acc * alpha + pv.reshape(Hg, TQ, dh)

    m0 = jnp.full((Hg, TQ, 1), NEG, f32)
    l0 = jnp.zeros((Hg, TQ, 1), f32)
    a0 = jnp.zeros((Hg, TQ, dh), f32)
    _, l, acc = lax.fori_loop(0, n_tiles, sel_tile, (m0, l0, a0))
    o_slc = acc / jnp.maximum(l, 1e-30)

    woff = pl.multiple_of(t0, TQ)
    k_w = kw_ref[pl.ds(woff, W + TQ), :]
    v_w = vw_ref[pl.ds(woff, W + TQ), :]
    s = lax.dot_general(q2, k_w, dn, preferred_element_type=f32)
    wpos = t0 - W + lax.broadcasted_iota(jnp.int32, (1, W + TQ), 1)
    dpos = tpos - wpos
    p = softmax_rows(s, (dpos >= 0) & (dpos < W) & (wpos >= 0))
    o_win = jnp.dot(p.reshape(Hg * TQ, W + TQ).astype(bf16), v_w, preferred_element_type=f32)

    g = g_ref[...]
    o_cmp = o_cmp.reshape(Hg, TQ, dh)
    o_win = o_win.reshape(Hg, TQ, dh)
    for h in range(Hg):
        o_ref[:, h * dh:(h + 1) * dh] = (g[:, h:h + 1] * o_cmp[h] + g[:, Hg + h:Hg + h + 1] * o_slc[h]
                                         + g[:, 2 * Hg + h:2 * Hg + h + 1] * o_win[h])


def nsa_attention(q, k_cmp, v_cmp, ks, vs, kw, vw, gates):
    B, G, Hg, T, dh = q.shape
    L, S_, SEL, W, TQ = NSA_CMP_BLOCK, NSA_CMP_STRIDE, NSA_SEL_BLOCK, NSA_WINDOW, NSA_TQ
    n_cmp = k_cmp.shape[2]
    n_sel = T // SEL
    top_n = min(NSA_TOPN, n_sel)
    ncp = -(-n_cmp // LANE) * LANE
    nsp = -(-n_sel // LANE) * LANE
    cmp_start = jnp.arange(n_cmp) * S_
    sel_start = jnp.arange(n_sel) * SEL
    overlap = jnp.clip(jnp.minimum(cmp_start[:, None] + L, sel_start[None, :] + SEL)
                       - jnp.maximum(cmp_start[:, None], sel_start[None, :]), 0).astype(f32) / L
    ovl = jnp.pad(overlap, ((0, ncp - n_cmp), (0, nsp - n_sel)))
    ovl = jnp.tile(ovl, (Hg, 1)).astype(bf16)
    expand = (jnp.arange(nsp)[:, None] == (jnp.arange(T)[None, :] // SEL)).astype(bf16)
    padc = ((0, 0), (0, 0), (0, ncp - n_cmp), (0, 0))
    padw = ((0, 0), (0, 0), (W, 0), (0, 0))
    g2 = gates.transpose(0, 2, 4, 1, 3).reshape(B, G, T, 3 * Hg)
    g2 = jnp.pad(g2, ((0, 0), (0, 0), (0, 0), (0, 16 - 3 * Hg)))

    def kv(n):
        return pl.BlockSpec((None, None, n, dh), lambda b, g, i: (b, g, 0, 0))

    return pl.pallas_call(
        functools.partial(_nsa_kernel, top_n=top_n),
        grid=(B, G, T // TQ),
        in_specs=[pl.BlockSpec((None, None, Hg, TQ, dh), lambda b, g, i: (b, g, 0, i, 0)),
                  kv(ncp), kv(ncp), kv(T), kv(T), kv(T + W), kv(T + W),
                  pl.BlockSpec((None, None, TQ, 16), lambda b, g, i: (b, g, i, 0)),
                  pl.BlockSpec((Hg * ncp, nsp), lambda b, g, i: (0, 0)),
                  pl.BlockSpec((nsp, T), lambda b, g, i: (0, 0))],
        out_specs=pl.BlockSpec((None, TQ, Hg * dh), lambda b, g, i: (b, i, g)),
        out_shape=jax.ShapeDtypeStruct((B, T, G * Hg * dh), f32),
        compiler_params=pltpu.CompilerParams(
            dimension_semantics=("parallel", "parallel", "arbitrary"), vmem_limit_bytes=56 * 1024 * 1024),
        name="nsa_attention",
    )(q.astype(bf16), jnp.pad(k_cmp, padc).astype(bf16), jnp.pad(v_cmp, padc).astype(bf16),
      ks.astype(bf16), vs.astype(bf16), jnp.pad(kw, padw).astype(bf16), jnp.pad(vw, padw).astype(bf16),
      g2, ovl, expand)


def nsa_mixer(h, w_in, q_gain, kc_gain, ks_gain, kw_gain, pe_k, w1_k, w2_k, pe_v, w1_v, w2_v, w_out):
    B, T, D = h.shape
    H, G, Hg, dh = NSA_HEADS, NSA_GROUPS, NSA_HPG, NSA_DH
    kvw = G * dh
    L, S_ = NSA_CMP_BLOCK, NSA_CMP_STRIDE
    n_cmp = (T - L) // S_ + 1

    proj = _proj(h, w_in)
    q = proj[..., :H * dh].reshape(B, T, H, dh)
    kc, vc, ks, vs, kw, vw = [proj[..., H * dh + j * kvw:H * dh + (j + 1) * kvw]
                              .reshape(B, T, G, dh).transpose(0, 2, 1, 3) for j in range(6)]
    gates = jax.nn.sigmoid(proj[..., H * dh + 6 * kvw:].astype(f32))
    gates = gates.reshape(B, T, 3, H).transpose(0, 2, 3, 1).reshape(B, 3, G, Hg, T)

    pos = jnp.arange(T)
    q = rotary(rms_norm(q, q_gain).transpose(0, 2, 1, 3), pos, ROPE_DIMS, ROPE_THETA)
    q = q.reshape(B, G, Hg, T, dh) * (dh ** -0.5)
    ks = rotary(rms_norm(ks, ks_gain), pos, ROPE_DIMS, ROPE_THETA)
    kw = rotary(rms_norm(kw, kw_gain), pos, ROPE_DIMS, ROPE_THETA)

    cmp_start = jnp.arange(n_cmp) * S_
    cmp_end = cmp_start + L - 1
    cmp_idx = cmp_start[:, None] + jnp.arange(L)[None, :]
    k_cmp = rotary(rms_norm(compress_blocks(kc, cmp_idx, pe_k, w1_k, w2_k), kc_gain), cmp_end,
                   ROPE_DIMS, ROPE_THETA)
    v_cmp = compress_blocks(vc, cmp_idx, pe_v, w1_v, w2_v)

    o = nsa_attention(q, k_cmp, v_cmp, ks, vs, kw, vw, gates)
    return _proj(o, w_out)


PEER_TM = 128
SC_LANES = 16
SC_WORKERS = 32
SC_ROWS = 32
SC_TOK_BLOCK = 8
PEER_PICKS = PEER_HEADS * PEER_TOPK


def _top_rows(s, k, payload=None):
    R = s.shape[0]
    rows = lax.broadcasted_iota(jnp.int32, s.shape, 0)
    vals, ids = [], []
    for _ in range(k):
        m = jnp.max(s, axis=0, keepdims=True)
        first = jnp.min(jnp.where(s == m, rows, R), axis=0, keepdims=True)
        hit = rows == first
        vals.append(m)
        if payload is None:
            ids.append(first)
        else:
            ids.append(jnp.max(jnp.where(hit, payload, -1), axis=0, keepdims=True))
        s = jnp.where(hit, -jnp.inf, s)
    return jnp.concatenate(vals, axis=0), jnp.concatenate(ids, axis=0)


def _peer_route_kernel(x_ref, g_ref, wq_ref, k1_ref, k2_ref, xn_ref, eid_ref, gate_ref, eid_t, gate_t):
    x = x_ref[...]
    xn = x * lax.rsqrt(jnp.mean(x * x, axis=-1, keepdims=True) + NORM_EPS) * g_ref[...]
    xn_ref[...] = xn
    q = jnp.dot(xn.astype(bf16), wq_ref[...], preferred_element_type=f32)
    half = PEER_QDIM // 2
    K = PEER_TOPK
    k1 = k1_ref[...]
    k2 = k2_ref[...]
    dn = (((1,), (1,)), ((), ()))
    for h in range(PEER_HEADS):
        q1 = q[:, h * PEER_QDIM:h * PEER_QDIM + half].astype(bf16)
        q2 = q[:, h * PEER_QDIM + half:(h + 1) * PEER_QDIM].astype(bf16)
        s1 = lax.dot_general(k1, q1, dn, preferred_element_type=f32)
        s2 = lax.dot_general(k2, q2, dn, preferred_element_type=f32)
        v1, i1 = _top_rows(s1, K)
        v2, i2 = _top_rows(s2, K)
        cand = jnp.concatenate([v1[i:i + 1, :] + v2 for i in range(K)], axis=0)
        cid = jnp.concatenate([i1[i:i + 1, :] * PEER_NKEYS + i2 for i in range(K)], axis=0)
        sc, eid = _top_rows(cand, K, payload=cid)
        e = jnp.exp(sc - jnp.max(sc, axis=0, keepdims=True))
        eid_t[h * K:(h + 1) * K, :] = eid
        gate_t[h * K:(h + 1) * K, :] = e / jnp.sum(e, axis=0, keepdims=True)
    eid_ref[...] = eid_t[...].T
    gate_ref[...] = gate_t[...].T


def _peer_route(x2, gain, w_q, keys1, keys2):
    N, D = x2.shape
    nq = PEER_HEADS * PEER_QDIM
    P = PEER_PICKS
    tm = PEER_TM
    return pl.pallas_call(
        _peer_route_kernel,
        grid=(N // tm,),
        in_specs=[pl.BlockSpec((tm, D), lambda i: (i, 0)),
                  pl.BlockSpec((1, D), lambda i: (0, 0)),
                  pl.BlockSpec((D, nq), lambda i: (0, 0)),
                  pl.BlockSpec(keys1.shape, lambda i: (0, 0)),
                  pl.BlockSpec(keys2.shape, lambda i: (0, 0))],
        out_specs=[pl.BlockSpec((tm, D), lambda i: (i, 0)),
                   pl.BlockSpec((tm, P), lambda i: (i, 0)),
                   pl.BlockSpec((tm, P), lambda i: (i, 0))],
        out_shape=[jax.ShapeDtypeStruct((N, D), f32),
                   jax.ShapeDtypeStruct((N, P), jnp.int32),
                   jax.ShapeDtypeStruct((N, P), f32)],
        scratch_shapes=[pltpu.VMEM((P, tm), jnp.int32), pltpu.VMEM((P, tm), f32)],
        compiler_params=pltpu.CompilerParams(
            dimension_semantics=("parallel",), vmem_limit_bytes=48 * 1024 * 1024),
        name="peer_route",
    )(x2, gain.reshape(1, D), w_q.astype(bf16), keys1.astype(bf16), keys2.astype(bf16))


def _sc_mesh():
    return plsc.VectorSubcoreMesh(core_axis_name="c", subcore_axis_name="s")


def _sc_gather_dot(u, idx, x2):
    N, D = x2.shape
    G = PEER_PICKS // SC_ROWS
    R, L = SC_ROWS, SC_LANES
    tpw = N // SC_WORKERS
    nblk = tpw // SC_TOK_BLOCK
    items = SC_TOK_BLOCK * G
    nch = D // L

    @functools.partial(
        pl.kernel, mesh=_sc_mesh(), compiler_params=pltpu.CompilerParams(needs_layout_passes=False),
        out_type=jax.ShapeDtypeStruct((N * G, R), f32),
        scratch_types=[
            pltpu.VMEM((items, R), jnp.int32),
            pltpu.VMEM((SC_TOK_BLOCK, D), f32),
            pltpu.VMEM((items, R), f32),
            pltpu.VMEM((R, D), f32),
            pltpu.VMEM((R, D), f32),
            pltpu.SemaphoreType.DMA,
            pltpu.SemaphoreType.DMA,
        ],
        name="peer_sc_dot",
    )
    def k(u_hbm, idx_hbm, x_hbm, h_hbm, idx_v, x_v, h_v, rows0, rows1, sem0, sem1):
        wid = lax.axis_index("s") * 2 + lax.axis_index("c")
        rows = (rows0, rows1)
        sems = (sem0, sem1)
        lane = lax.iota(jnp.int32, L)

        def gather(item, b):
            return pltpu.make_async_copy(u_hbm.at[idx_v.at[item]], rows[b], sems[b])

        def compute(item, b):
            tl = item // G
            rv = rows[b]

            def chunk(c, accs):
                off = pl.multiple_of(c * L, L)
                xc = x_v[tl, pl.ds(off, L)]
                return tuple(accs[r] + rv[r, pl.ds(off, L)] * xc for r in range(R))

            accs = lax.fori_loop(0, nch, chunk, tuple(jnp.zeros((L,), f32) for _ in range(R)))
            for half in range(R // L):
                out = jnp.zeros((L,), f32)
                for r in range(L):
                    out = jnp.where(lane == r, jnp.sum(accs[half * L + r]), out)
                h_v[item, pl.ds(half * L, L)] = out

        @pl.loop(0, nblk)
        def _(blk):
            tok0 = wid * tpw + blk * SC_TOK_BLOCK
            it0 = tok0 * G
            pltpu.sync_copy(idx_hbm.at[pl.ds(it0, items)], idx_v)
            pltpu.sync_copy(x_hbm.at[pl.ds(tok0, SC_TOK_BLOCK)], x_v)
            gather(0, 0).start()

            @pl.loop(0, items, step=2)
            def _(it):
                gather(it + 1, 1).start()
                gather(it, 0).wait()
                compute(it, 0)

                @pl.when(it + 2 < items)
                def _():
                    gather(it + 2, 0).start()

                gather(it + 1, 1).wait()
                compute(it + 1, 1)

            pltpu.sync_copy(h_v, h_hbm.at[pl.ds(it0, items)])

    return k(u, idx, x2)


def _sc_gather_wsum(v, idx, w, N):
    D = v.shape[1]
    G = PEER_PICKS // SC_ROWS
    R, L = SC_ROWS, SC_LANES
    tpw = N // SC_WORKERS
    nblk = tpw // SC_TOK_BLOCK
    items = SC_TOK_BLOCK * G
    nch = D // L

    @functools.partial(
        pl.kernel, mesh=_sc_mesh(), compiler_params=pltpu.CompilerParams(needs_layout_passes=False),
        out_type=jax.ShapeDtypeStruct((N, D), f32),
        scratch_types=[
            pltpu.VMEM((items, R), jnp.int32),
            pltpu.VMEM((items, R), f32),
            pltpu.VMEM((SC_TOK_BLOCK, D), f32),
            pltpu.VMEM((R, D), f32),
            pltpu.VMEM((R, D), f32),
            pltpu.SemaphoreType.DMA,
            pltpu.SemaphoreType.DMA,
        ],
        name="peer_sc_wsum",
    )
    def k(v_hbm, idx_hbm, w_hbm, o_hbm, idx_v, w_v, o_v, rows0, rows1, sem0, sem1):
        wid = lax.axis_index("s") * 2 + lax.axis_index("c")
        rows = (rows0, rows1)
        sems = (sem0, sem1)
        lane = lax.iota(jnp.int32, L)

        def gather(item, b):
            return pltpu.make_async_copy(v_hbm.at[idx_v.at[item]], rows[b], sems[b])

        def compute(item, b):
            tl = item // G
            first = (item % G) == 0
            rv = rows[b]
            for half in range(R // L):
                wv = w_v[item, pl.ds(half * L, L)]
                ws = [jnp.sum(jnp.where(lane == r, wv, 0.0)) for r in range(L)]

                @plsc.parallel_loop(0, nch, unroll=2)
                def _(c):
                    off = pl.multiple_of(c * L, L)
                    prods = [rv[half * L + r, pl.ds(off, L)] * ws[r] for r in range(L)]
                    while len(prods) > 1:
                        prods = [a + b for a, b in zip(prods[::2], prods[1::2])]
                    prev = o_v[tl, pl.ds(off, L)]
                    if half == 0:
                        prev = jnp.where(first, 0.0, prev)
                    o_v[tl, pl.ds(off, L)] = prods[0] + prev

        @pl.loop(0, nblk)
        def _(blk):
            tok0 = wid * tpw + blk * SC_TOK_BLOCK
            it0 = tok0 * G
            pltpu.sync_copy(idx_hbm.at[pl.ds(it0, items)], idx_v)
            pltpu.sync_copy(w_hbm.at[pl.ds(it0, items)], w_v)
            gather(0, 0).start()

            @pl.loop(0, items, step=2)
            def _(it):
                gather(it + 1, 1).start()
                gather(it, 0).wait()
                compute(it, 0)

                @pl.when(it + 2 < items)
                def _():
                    gather(it + 2, 0).start()

                gather(it + 1, 1).wait()
                compute(it + 1, 1)

            pltpu.sync_copy(o_v, o_hbm.at[pl.ds(tok0, SC_TOK_BLOCK)])

    return k(v, idx, w)


def peer_ffn(x, gain, w_q, keys1, keys2, u, v):
    B, T, D = x.shape
    N = B * T
    G = PEER_PICKS // SC_ROWS
    xn, eid, gate = _peer_route(x.reshape(N, D), gain, w_q, keys1, keys2)
    eid = eid.reshape(N * G, SC_ROWS)
    gate = gate.reshape(N * G, SC_ROWS)
    h = _sc_gather_dot(u, eid, xn)
    w = gate * jax.nn.gelu(h, approximate=False)
    return _sc_gather_wsum(v, eid, w, N).reshape(B, T, D)


def kernel(x,
           l0_attn_norm, l0_ret_w_in, l0_ret_out_gain, l0_ret_w_out,
           l0_ffn_norm, l0_peer_w_q, l0_peer_keys1, l0_peer_keys2, l0_peer_u, l0_peer_v,
           l1_attn_norm, l1_gdn_w_in, l1_gdn_conv_w, l1_gdn_a_log, l1_gdn_dt_bias, l1_gdn_out_gain, l1_gdn_w_out,
           l1_ffn_norm, l1_peer_w_q, l1_peer_keys1, l1_peer_keys2, l1_peer_u, l1_peer_v,
           l2_attn_norm, l2_nsa_w_in, l2_nsa_q_gain, l2_nsa_kc_gain, l2_nsa_ks_gain, l2_nsa_kw_gain,
           l2_nsa_pe_k, l2_nsa_w1_k, l2_nsa_w2_k, l2_nsa_pe_v, l2_nsa_w1_v, l2_nsa_w2_v, l2_nsa_w_out,
           l2_ffn_norm, l2_peer_w_q, l2_peer_keys1, l2_peer_keys2, l2_peer_u, l2_peer_v,
           l3_attn_norm, l3_ret_w_in, l3_ret_out_gain, l3_ret_w_out,
           l3_ffn_norm, l3_peer_w_q, l3_peer_keys1, l3_peer_keys2, l3_peer_u, l3_peer_v):
    mixers = (retention_mixer, gated_deltanet_mixer, nsa_mixer)
    attn_norms = (l0_attn_norm, l1_attn_norm, l2_attn_norm, l3_attn_norm)
    mixer_args = (
        (l0_ret_w_in, l0_ret_out_gain, l0_ret_w_out),
        (l1_gdn_w_in, l1_gdn_conv_w, l1_gdn_a_log, l1_gdn_dt_bias, l1_gdn_out_gain, l1_gdn_w_out),
        (l2_nsa_w_in, l2_nsa_q_gain, l2_nsa_kc_gain, l2_nsa_ks_gain, l2_nsa_kw_gain,
         l2_nsa_pe_k, l2_nsa_w1_k, l2_nsa_w2_k, l2_nsa_pe_v, l2_nsa_w1_v, l2_nsa_w2_v, l2_nsa_w_out),
        (l3_ret_w_in, l3_ret_out_gain, l3_ret_w_out),
    )
    ffn_norms = (l0_ffn_norm, l1_ffn_norm, l2_ffn_norm, l3_ffn_norm)
    peer_args = (
        (l0_peer_w_q, l0_peer_keys1, l0_peer_keys2, l0_peer_u, l0_peer_v),
        (l1_peer_w_q, l1_peer_keys1, l1_peer_keys2, l1_peer_u, l1_peer_v),
        (l2_peer_w_q, l2_peer_keys1, l2_peer_keys2, l2_peer_u, l2_peer_v),
        (l3_peer_w_q, l3_peer_keys1, l3_peer_keys2, l3_peer_u, l3_peer_v),
    )
    for i in range(DEPTH):
        x = x + mixers[i % MIXER_CYCLE](rms_norm(x, attn_norms[i]), *mixer_args[i])
        x = x + peer_ffn(x, ffn_norms[i], *peer_args[i])
    return x
```

```python
import functools
import math

import jax
import jax.numpy as jnp
from jax import lax
from jax.experimental import pallas as pl
from jax.experimental.pallas import tpu as pltpu
from jax.experimental.pallas import tpu_sc as plsc

D_MODEL = 1024
DEPTH = 4
f32 = jnp.float32
bf16 = jnp.bfloat16
NORM_EPS = 1e-6
MIXER_CYCLE = 3

RET_HEADS = 8
RET_DK = D_MODEL // RET_HEADS
RET_DV = D_MODEL // RET_HEADS
RET_CHUNK = 128
RET_THETA = 10000.0

GDN_HEADS = 8
GDN_DK = D_MODEL // GDN_HEADS
GDN_DV = D_MODEL // GDN_HEADS
GDN_CONV = 4
GDN_CHUNK = 64

NSA_HEADS = 16
NSA_GROUPS = 4
NSA_HPG = NSA_HEADS // NSA_GROUPS
NSA_DH = D_MODEL // NSA_HEADS
NSA_CMP_BLOCK = 32
NSA_CMP_STRIDE = 16
NSA_CMP_HIDDEN = 256
NSA_SEL_BLOCK = 64
NSA_TOPN = 16
NSA_WINDOW = 512
NSA_QBLOCK = 32
NSA_FORCE = 1e4
ROPE_THETA = 500000.0
ROPE_DIMS = NSA_DH // 4

PEER_NKEYS = 128
PEER_EXPERTS = PEER_NKEYS * PEER_NKEYS
PEER_HEADS = 8
PEER_QDIM = 256
PEER_TOPK = 16
PEER_TOKEN_BLOCK = 128


def _mm_kernel(a_ref, b_ref, o_ref):
    o_ref[...] = jnp.dot(a_ref[...].astype(bf16), b_ref[...].astype(bf16),
                         preferred_element_type=f32)


def _mm(a, b, tm=512, tn=1024):
    M, K = a.shape
    _, N = b.shape
    tn = min(tn, N)
    if N % tn:
        tn = N
    return pl.pallas_call(
        _mm_kernel,
        grid=(M // tm, N // tn),
        in_specs=[pl.BlockSpec((tm, K), lambda i, j: (i, 0)),
                  pl.BlockSpec((K, tn), lambda i, j: (0, j))],
        out_specs=pl.BlockSpec((tm, tn), lambda i, j: (i, j)),
        out_shape=jax.ShapeDtypeStruct((M, N), f32),
        compiler_params=pltpu.CompilerParams(
            dimension_semantics=("parallel", "parallel"),
            vmem_limit_bytes=48 * 1024 * 1024),
    )(a, b)


def _proj(h, w):
    B, T, D = h.shape
    return _mm(h.reshape(B * T, D), w).reshape(B, T, -1)


def rms_norm(x, gain):
    xf = x.astype(f32)
    y = xf * lax.rsqrt(jnp.mean(xf * xf, axis=-1, keepdims=True) + NORM_EPS)
    return (y * gain.astype(f32)).astype(x.dtype)


def l2_norm(x):
    xf = x.astype(f32)
    return xf * lax.rsqrt(jnp.sum(xf * xf, axis=-1, keepdims=True) + NORM_EPS)


def rotary(x, pos, rot_dims, theta):
    half = rot_dims // 2
    inv_freq = jnp.power(theta, -jnp.arange(half, dtype=f32) / half)
    ang = pos.astype(f32)[:, None] * inv_freq[None, :]
    cos, sin = jnp.cos(ang), jnp.sin(ang)
    xf = x.astype(f32)
    x1, x2 = xf[..., :half], xf[..., half:rot_dims]
    out = jnp.concatenate([x1 * cos - x2 * sin, x2 * cos + x1 * sin, xf[..., rot_dims:]], axis=-1)
    return out.astype(x.dtype)


def masked_softmax(s, mask):
    s = jnp.where(mask, s.astype(f32), -1e30)
    m = jnp.max(s, axis=-1, keepdims=True)
    e = jnp.where(mask, jnp.exp(s - m), 0.0)
    return e / jnp.maximum(jnp.sum(e, axis=-1, keepdims=True), 1e-30)


def to_heads(z, n_heads):
    B, T, _ = z.shape
    return z.reshape(B, T, n_heads, -1).transpose(0, 2, 1, 3)


def causal_depthwise_conv(x, w):
    K, C = w.shape
    return lax.conv_general_dilated(x, w.astype(x.dtype)[:, None, :], window_strides=(1,),
                                    padding=[(K - 1, 0)], dimension_numbers=('NWC', 'WIO', 'NWC'),
                                    feature_group_count=C)


def retention_mixer(h, w_in, out_gain, w_out):
    B, T, D = h.shape
    H, dk, dv, C = RET_HEADS, RET_DK, RET_DV, RET_CHUNK
    n_chunks = T // C
    q, k, v, g = jnp.split(_proj(h, w_in), 4, axis=-1)
    pos = jnp.arange(T)
    q = rotary(to_heads(q, H), pos, dk, RET_THETA)
    k = rotary(to_heads(k, H), pos, dk, RET_THETA) * (dk ** -0.5)
    v = to_heads(v, H)

    def chunks(z):
        return z.astype(f32).reshape(B, H, n_chunks, C, -1).transpose(2, 0, 1, 3, 4)

    log_gamma = jnp.log1p(-jnp.power(2.0, -5.0 - jnp.arange(H, dtype=f32)))
    idx = jnp.arange(C, dtype=f32)
    rel = idx[:, None] - idx[None, :]
    intra = jnp.where(rel >= 0, jnp.exp(log_gamma[:, None, None] * jnp.maximum(rel, 0.0)), 0.0)
    q_decay = jnp.exp(log_gamma[:, None] * (idx + 1.0))[:, :, None]
    k_decay = jnp.exp(log_gamma[:, None] * (C - 1.0 - idx))[:, :, None]
    chunk_decay = jnp.exp(log_gamma * C)[:, None, None]

    def step(state, qkv):
        qc, kc, vc = qkv
        s = jnp.einsum('bhid,bhjd->bhij', qc, kc) * intra
        o = (jnp.einsum('bhij,bhjv->bhiv', s, vc)
             + jnp.einsum('bhid,bhdv->bhiv', qc, state) * q_decay)
        state = state * chunk_decay + jnp.einsum('bhjd,bhjv->bhdv', kc * k_decay, vc)
        return state, o

    state0 = jnp.zeros((B, H, dk, dv), f32)
    _, o = lax.scan(step, state0, (chunks(q), chunks(k), chunks(v)))
    o = o.transpose(1, 0, 3, 2, 4).reshape(B, T, H, dv)
    o = rms_norm(o, out_gain.reshape(H, dv)).reshape(B, T, H * dv).astype(h.dtype)
    return _proj(jax.nn.silu(g) * o, w_out)


def gated_deltanet_mixer(h, w_in, conv_w, a_log, dt_bias, out_gain, w_out):
    B, T, D = h.shape
    H, dk, dv, C = GDN_HEADS, GDN_DK, GDN_DV, GDN_CHUNK
    n_qk, n_v = H * dk, H * dv
    n_conv = 2 * n_qk + n_v
    n = T // C
    proj = jnp.dot(h, w_in)
    qkv = jax.nn.silu(causal_depthwise_conv(proj[..., :n_conv], conv_w))
    z = proj[..., n_conv:n_conv + n_v]
    beta = jax.nn.sigmoid(proj[..., n_conv + n_v:n_conv + n_v + H].astype(f32))
    a = proj[..., n_conv + n_v + H:].astype(f32)
    g = -jnp.exp(a_log.astype(f32)) * jax.nn.softplus(a + dt_bias.astype(f32))

    def chunks(zz):
        return zz.reshape(B, H, n, C, -1)

    q = chunks(l2_norm(to_heads(qkv[..., :n_qk], H)) * (dk ** -0.5))
    k = chunks(l2_norm(to_heads(qkv[..., n_qk:2 * n_qk], H)))
    v = chunks(to_heads(qkv[..., 2 * n_qk:], H).astype(f32))
    beta = beta.transpose(0, 2, 1).reshape(B, H, n, C)
    g_cum = jnp.cumsum(g.transpose(0, 2, 1).reshape(B, H, n, C), axis=-1)

    tril = jnp.tril(jnp.ones((C, C), bool))
    strict = jnp.tril(jnp.ones((C, C), f32), -1)
    diff = g_cum[..., :, None] - g_cum[..., None, :]
    decay = jnp.where(tril, jnp.exp(jnp.minimum(diff, 0.0)), 0.0)
    k_beta = k * beta[..., None]
    lower = jnp.einsum('bhnid,bhnjd->bhnij', k_beta, k) * decay * strict
    eye = jnp.eye(C, dtype=f32)
    t_mat = lax.linalg.triangular_solve(eye + lower, jnp.broadcast_to(eye, lower.shape),
                                        left_side=True, lower=True)
    u = jnp.einsum('bhnij,bhnjv->bhniv', t_mat, v * beta[..., None])
    w = jnp.einsum('bhnij,bhnjd->bhnid', t_mat, k_beta * jnp.exp(g_cum)[..., None])
    qk = jnp.einsum('bhnid,bhnjd->bhnij', q, k) * decay
    q_dec = q * jnp.exp(g_cum)[..., None]
    g_last = g_cum[..., -1]
    k_dec = k * jnp.exp(g_last[..., None] - g_cum)[..., None]

    def mv(zz):
        return jnp.moveaxis(zz, 2, 0)

    def step(S, inp):
        qk_i, q_dec_i, k_dec_i, u_i, w_i, g_last_i = inp
        v_new = u_i - jnp.einsum('bhcd,bhdv->bhcv', w_i, S)
        o = jnp.einsum('bhcd,bhdv->bhcv', q_dec_i, S) + jnp.einsum('bhij,bhjv->bhiv', qk_i, v_new)
        S = S * jnp.exp(g_last_i)[..., None, None] + jnp.einsum('bhcd,bhcv->bhdv', k_dec_i, v_new)
        return S, o

    S0 = jnp.zeros((B, H, dk, dv), f32)
    _, o = lax.scan(step, S0, (mv(qk), mv(q_dec), mv(k_dec), mv(u), mv(w), jnp.moveaxis(g_last, 2, 0)))
    o = o.transpose(1, 0, 3, 2, 4).reshape(B, T, H, dv)
    o = rms_norm(o, out_gain) * jax.nn.silu(z.reshape(B, T, H, dv).astype(f32))
    return _proj(o.reshape(B, T, H * dv).astype(h.dtype), w_out)


def compress_blocks(zz, idx, pe, w1, w2):
    blocks = zz[:, :, idx] + pe
    flat = blocks.reshape(blocks.shape[0], blocks.shape[1], blocks.shape[2], -1)
    return jax.nn.gelu(flat @ w1, approximate=False) @ w2


NSA_TQ = 128
NSA_TK = 512
LANE = 128
NEG = -1e30


def _nsa_kernel(q_ref, kc_ref, vc_ref, ks_ref, vs_ref, kw_ref, vw_ref, g_ref, ovl_ref, exp_ref, o_ref, *, top_n):
    Hg, TQ, dh = q_ref.shape
    TK, W = NSA_TK, NSA_WINDOW
    ncp = kc_ref.shape[0]
    nsp = exp_ref.shape[0]
    t0 = pl.program_id(2) * TQ
    q2 = q_ref[...].reshape(Hg * TQ, dh)
    tpos = t0 + lax.broadcasted_iota(jnp.int32, (TQ, 1), 0)
    dn = (((1,), (1,)), ((), ()))

    def softmax_rows(s, mask):
        s3 = jnp.where(mask[None], s.reshape(Hg, TQ, -1), NEG)
        m = jnp.max(s3, axis=-1, keepdims=True)
        e = jnp.where(mask[None], jnp.exp(s3 - m), 0.0)
        return e / jnp.maximum(jnp.sum(e, axis=-1, keepdims=True), 1e-30)

    s = lax.dot_general(q2, kc_ref[...], dn, preferred_element_type=f32)
    cend = lax.broadcasted_iota(jnp.int32, (1, ncp), 1) * NSA_CMP_STRIDE + (NSA_CMP_BLOCK - 1)
    p = softmax_rows(s, cend <= tpos)
    o_cmp = jnp.dot(p.reshape(Hg * TQ, ncp).astype(bf16), vc_ref[...], preferred_element_type=f32)

    p_all = jnp.concatenate([p[h] for h in range(Hg)], axis=-1).astype(bf16)
    imp = jnp.dot(p_all, ovl_ref[...], preferred_element_type=f32)
    blk = lax.broadcasted_iota(jnp.int32, (1, nsp), 1)
    cur = tpos // NSA_SEL_BLOCK
    forced = (blk == 0) | (blk == cur) | (blk == cur - 1)
    score = jnp.where(forced, NSA_FORCE, jnp.where(blk <= cur, imp, -NSA_FORCE))
    n_real = exp_ref.shape[1] // NSA_SEL_BLOCK
    score = jnp.where(blk < n_real, score, -jnp.inf)
    st = score.T
    rows = lax.broadcasted_iota(jnp.int32, st.shape, 0)
    sel_t = jnp.zeros(st.shape, f32)
    for _ in range(top_n):
        m = jnp.max(st, axis=0, keepdims=True)
        first = jnp.min(jnp.where(st == m, rows, nsp), axis=0, keepdims=True)
        hit = rows == first
        sel_t = jnp.where(hit, 1.0, sel_t)
        st = jnp.where(hit, -jnp.inf, st)
    sel = sel_t.T.astype(bf16)

    n_tiles = (t0 + TQ + TK - 1) // TK

    def sel_tile(kt, carry):
        m, l, acc = carry
        koff = pl.multiple_of(kt * TK, TK)
        k_t = ks_ref[pl.ds(koff, TK), :]
        v_t = vs_ref[pl.ds(koff, TK), :]
        s = lax.dot_general(q2, k_t, dn, preferred_element_type=f32)
        picked = jnp.dot(sel, exp_ref[:, pl.ds(koff, TK)], preferred_element_type=f32)
        kpos = koff + lax.broadcasted_iota(jnp.int32, (1, TK), 1)
        mask = ((picked > 0.5) & (kpos <= tpos))[None]
        s3 = jnp.where(mask, s.reshape(Hg, TQ, TK), NEG)
        m_new = jnp.maximum(m, jnp.max(s3, axis=-1, keepdims=True))
        alpha = jnp.exp(m - m_new)
        e = jnp.where(mask, jnp.exp(s3 - m_new), 0.0)
        l = l * alpha + jnp.sum(e, axis=-1, keepdims=True)
        pv = jnp.dot(e.reshape(Hg * TQ, TK).astype(bf16), v_t, preferred_element_type=f32)
        return m_new, l, acc * alpha + pv.reshape(Hg, TQ, dh)

    m0 = jnp.full((Hg, TQ, 1), NEG, f32)
    l0 = jnp.zeros((Hg, TQ, 1), f32)
    a0 = jnp.zeros((Hg, TQ, dh), f32)
    _, l, acc = lax.fori_loop(0, n_tiles, sel_tile, (m0, l0, a0))
    o_slc = acc / jnp.maximum(l, 1e-30)

    woff = pl.multiple_of(t0, TQ)
    k_w = kw_ref[pl.ds(woff, W + TQ), :]
    v_w = vw_ref[pl.ds(woff, W + TQ), :]
    s = lax.dot_general(q2, k_w, dn, preferred_element_type=f32)
    wpos = t0 - W + lax.broadcasted_iota(jnp.int32, (1, W + TQ), 1)
    dpos = tpos - wpos
    p = softmax_rows(s, (dpos >= 0) & (dpos < W) & (wpos >= 0))
    o_win = jnp.dot(p.reshape(Hg * TQ, W + TQ).astype(bf16), v_w, preferred_element_type=f32)

    g = g_ref[...]
    o_cmp = o_cmp.reshape(Hg, TQ, dh)
    o_win = o_win.reshape(Hg, TQ, dh)
    for h in range(Hg):
        o_ref[:, h * dh:(h + 1) * dh] = (g[:, h:h + 1] * o_cmp[h] + g[:, Hg + h:Hg + h + 1] * o_slc[h]
                                         + g[:, 2 * Hg + h:2 * Hg + h + 1] * o_win[h])


def nsa_attention(q, k_cmp, v_cmp, ks, vs, kw, vw, gates):
    B, G, Hg, T, dh = q.shape
    L, S_, SEL, W, TQ = NSA_CMP_BLOCK, NSA_CMP_STRIDE, NSA_SEL_BLOCK, NSA_WINDOW, NSA_TQ
    n_cmp = k_cmp.shape[2]
    n_sel = T // SEL
    top_n = min(NSA_TOPN, n_sel)
    ncp = -(-n_cmp // LANE) * LANE
    nsp = -(-n_sel // LANE) * LANE
    cmp_start = jnp.arange(n_cmp) * S_
    sel_start = jnp.arange(n_sel) * SEL
    overlap = jnp.clip(jnp.minimum(cmp_start[:, None] + L, sel_start[None, :] + SEL)
                       - jnp.maximum(cmp_start[:, None], sel_start[None, :]), 0).astype(f32) / L
    ovl = jnp.pad(overlap, ((0, ncp - n_cmp), (0, nsp - n_sel)))
    ovl = jnp.tile(ovl, (Hg, 1)).astype(bf16)
    expand = (jnp.arange(nsp)[:, None] == (jnp.arange(T)[None, :] // SEL)).astype(bf16)
    padc = ((0, 0), (0, 0), (0, ncp - n_cmp), (0, 0))
    padw = ((0, 0), (0, 0), (W, 0), (0, 0))
    g2 = gates.transpose(0, 2, 4, 1, 3).reshape(B, G, T, 3 * Hg)
    g2 = jnp.pad(g2, ((0, 0), (0, 0), (0, 0), (0, 16 - 3 * Hg)))

    def kv(n):
        return pl.BlockSpec((None, None, n, dh), lambda b, g, i: (b, g, 0, 0))

    return pl.pallas_call(
        functools.partial(_nsa_kernel, top_n=top_n),
        grid=(B, G, T // TQ),
        in_specs=[pl.BlockSpec((None, None, Hg, TQ, dh), lambda b, g, i: (b, g, 0, i, 0)),
                  kv(ncp), kv(ncp), kv(T), kv(T), kv(T + W), kv(T + W),
                  pl.BlockSpec((None, None, TQ, 16), lambda b, g, i: (b, g, i, 0)),
                  pl.BlockSpec((Hg * ncp, nsp), lambda b, g, i: (0, 0)),
                  pl.BlockSpec((nsp, T), lambda b, g, i: (0, 0))],
        out_specs=pl.BlockSpec((None, TQ, Hg * dh), lambda b, g, i: (b, i, g)),
        out_shape=jax.ShapeDtypeStruct((B, T, G * Hg * dh), f32),
        compiler_params=pltpu.CompilerParams(
            dimension_semantics=("parallel", "parallel", "arbitrary"), vmem_limit_bytes=56 * 1024 * 1024),
        name="nsa_attention",
    )(q.astype(bf16), jnp.pad(k_cmp, padc).astype(bf16), jnp.pad(v_cmp, padc).astype(bf16),
      ks.astype(bf16), vs.astype(bf16), jnp.pad(kw, padw).astype(bf16), jnp.pad(vw, padw).astype(bf16),
      g2, ovl, expand)


def nsa_mixer(h, w_in, q_gain, kc_gain, ks_gain, kw_gain, pe_k, w1_k, w2_k, pe_v, w1_v, w2_v, w_out):
    B, T, D = h.shape
    H, G, Hg, dh = NSA_HEADS, NSA_GROUPS, NSA_HPG, NSA_DH
    kvw = G * dh
    L, S_ = NSA_CMP_BLOCK, NSA_CMP_STRIDE
    n_cmp = (T - L) // S_ + 1

    proj = _proj(h, w_in)
    q = proj[..., :H * dh].reshape(B, T, H, dh)
    kc, vc, ks, vs, kw, vw = [proj[..., H * dh + j * kvw:H * dh + (j + 1) * kvw]
                              .reshape(B, T, G, dh).transpose(0, 2, 1, 3) for j in range(6)]
    gates = jax.nn.sigmoid(proj[..., H * dh + 6 * kvw:].astype(f32))
    gates = gates.reshape(B, T, 3, H).transpose(0, 2, 3, 1).reshape(B, 3, G, Hg, T)

    pos = jnp.arange(T)
    q = rotary(rms_norm(q, q_gain).transpose(0, 2, 1, 3), pos, ROPE_DIMS, ROPE_THETA)
    q = q.reshape(B, G, Hg, T, dh) * (dh ** -0.5)
    ks = rotary(rms_norm(ks, ks_gain), pos, ROPE_DIMS, ROPE_THETA)
    kw = rotary(rms_norm(kw, kw_gain), pos, ROPE_DIMS, ROPE_THETA)

    cmp_start = jnp.arange(n_cmp) * S_
    cmp_end = cmp_start + L - 1
    cmp_idx = cmp_start[:, None] + jnp.arange(L)[None, :]
    k_cmp = rotary(rms_norm(compress_blocks(kc, cmp_idx, pe_k, w1_k, w2_k), kc_gain), cmp_end,
                   ROPE_DIMS, ROPE_THETA)
    v_cmp = compress_blocks(vc, cmp_idx, pe_v, w1_v, w2_v)

    o = nsa_attention(q, k_cmp, v_cmp, ks, vs, kw, vw, gates)
    return _proj(o, w_out)


PEER_TM = 128
SC_LANES = 16
SC_WORKERS = 32
SC_ROWS = 32
SC_TOK_BLOCK = 8
PEER_PICKS = PEER_HEADS * PEER_TOPK
BATCH_CHAINS = 2


def _top_rows(s, k, order=None, payload=None):
    if order is None:
        order = lax.broadcasted_iota(jnp.int32, s.shape, 0)
    big = jnp.int32(2 ** 30)
    vals, ids = [], []
    for _ in range(k):
        m = jnp.max(s, axis=0, keepdims=True)
        first = jnp.min(jnp.where(s == m, order, big), axis=0, keepdims=True)
        hit = order == first
        vals.append(m)
        if payload is None:
            ids.append(first)
        else:
            ids.append(jnp.max(jnp.where(hit, payload, -1), axis=0, keepdims=True))
        s = jnp.where(hit, -jnp.inf, s)
    return jnp.concatenate(vals, axis=0), jnp.concatenate(ids, axis=0)


def _pair_candidates(v1, i1, v2, i2):
    K, S = PEER_TOPK, 8
    rows8 = lax.broadcasted_iota(jnp.int32, (S, v1.shape[1]), 0)
    vals = [v1[i:i + 1, :] + v2[0:S, :] for i in range(S)]
    flat = [i * K + rows8 for i in range(S)]
    cid = [i1[i:i + 1, :] * PEER_NKEYS + i2[0:S, :] for i in range(S)]
    vals.append(v1[0:1, :] + v2[S:K, :])
    flat.append(S + rows8)
    cid.append(i1[0:1, :] * PEER_NKEYS + i2[S:K, :])
    vals.append(v1[S:K, :] + v2[0:1, :])
    flat.append((S + rows8) * K)
    cid.append(i1[S:K, :] * PEER_NKEYS + i2[0:1, :])
    return tuple(jnp.concatenate(xs, axis=0) for xs in (vals, flat, cid))


def _peer_route_kernel(x_ref, g_ref, wq_ref, k1_ref, k2_ref, xn_ref, eid_ref, gate_ref, eid_t, gate_t):
    x = x_ref[...]
    xn = x * lax.rsqrt(jnp.mean(x * x, axis=-1, keepdims=True) + NORM_EPS) * g_ref[...]
    xn_ref[...] = xn
    q = jnp.dot(xn.astype(bf16), wq_ref[...], preferred_element_type=f32)
    half = PEER_QDIM // 2
    K = PEER_TOPK
    k1 = k1_ref[...]
    k2 = k2_ref[...]
    dn = (((1,), (1,)), ((), ()))
    for h in range(PEER_HEADS):
        q1 = q[:, h * PEER_QDIM:h * PEER_QDIM + half].astype(bf16)
        q2 = q[:, h * PEER_QDIM + half:(h + 1) * PEER_QDIM].astype(bf16)
        s1 = lax.dot_general(k1, q1, dn, preferred_element_type=f32)
        s2 = lax.dot_general(k2, q2, dn, preferred_element_type=f32)
        v1, i1 = _top_rows(s1, K)
        v2, i2 = _top_rows(s2, K)
        cand, flat, cid = _pair_candidates(v1, i1, v2, i2)
        sc, eid = _top_rows(cand, K, order=flat, payload=cid)
        e = jnp.exp(sc - jnp.max(sc, axis=0, keepdims=True))
        eid_t[h * K:(h + 1) * K, :] = eid
        gate_t[h * K:(h + 1) * K, :] = e / jnp.sum(e, axis=0, keepdims=True)
    eid_ref[...] = eid_t[...].T
    gate_ref[...] = gate_t[...].T


def _peer_route(x2, gain, w_q, keys1, keys2):
    N, D = x2.shape
    nq = PEER_HEADS * PEER_QDIM
    P = PEER_PICKS
    tm = PEER_TM
    return pl.pallas_call(
        _peer_route_kernel,
        grid=(N // tm,),
        in_specs=[pl.BlockSpec((tm, D), lambda i: (i, 0)),
                  pl.BlockSpec((1, D), lambda i: (0, 0)),
                  pl.BlockSpec((D, nq), lambda i: (0, 0)),
                  pl.BlockSpec(keys1.shape, lambda i: (0, 0)),
                  pl.BlockSpec(keys2.shape, lambda i: (0, 0))],
        out_specs=[pl.BlockSpec((tm, D), lambda i: (i, 0)),
                   pl.BlockSpec((tm, P), lambda i: (i, 0)),
                   pl.BlockSpec((tm, P), lambda i: (i, 0))],
        out_shape=[jax.ShapeDtypeStruct((N, D), f32),
                   jax.ShapeDtypeStruct((N, P), jnp.int32),
                   jax.ShapeDtypeStruct((N, P), f32)],
        scratch_shapes=[pltpu.VMEM((P, tm), jnp.int32), pltpu.VMEM((P, tm), f32)],
        compiler_params=pltpu.CompilerParams(
            dimension_semantics=("parallel",), vmem_limit_bytes=48 * 1024 * 1024),
        name="peer_route",
    )(x2, gain.reshape(1, D), w_q.astype(bf16), keys1.astype(bf16), keys2.astype(bf16))


def _sc_mesh():
    return plsc.VectorSubcoreMesh(core_axis_name="c", subcore_axis_name="s")


def _sc_gather_dot(u, idx, x2):
    N, D = x2.shape
    G = PEER_PICKS // SC_ROWS
    R, L = SC_ROWS, SC_LANES
    tpw = N // SC_WORKERS
    nblk = tpw // SC_TOK_BLOCK
    items = SC_TOK_BLOCK * G
    nch = D // L

    @functools.partial(
        pl.kernel, mesh=_sc_mesh(), compiler_params=pltpu.CompilerParams(needs_layout_passes=False),
        out_type=jax.ShapeDtypeStruct((N * G, R), f32),
        scratch_types=[
            pltpu.VMEM((items, R), jnp.int32),
            pltpu.VMEM((SC_TOK_BLOCK, D), f32),
            pltpu.VMEM((items, R), f32),
            pltpu.VMEM((R, D), f32),
            pltpu.VMEM((R, D), f32),
            pltpu.SemaphoreType.DMA,
            pltpu.SemaphoreType.DMA,
        ],
        name="peer_sc_dot",
    )
    def k(u_hbm, idx_hbm, x_hbm, h_hbm, idx_v, x_v, h_v, rows0, rows1, sem0, sem1):
        wid = lax.axis_index("s") * 2 + lax.axis_index("c")
        rows = (rows0, rows1)
        sems = (sem0, sem1)
        lane = lax.iota(jnp.int32, L)

        def gather(item, b):
            return pltpu.make_async_copy(u_hbm.at[idx_v.at[item]], rows[b], sems[b])

        def compute(item, b):
            tl = item // G
            rv = rows[b]

            def chunk(c, accs):
                off = pl.multiple_of(c * L, L)
                xc = x_v[tl, pl.ds(off, L)]
                return tuple(accs[r] + rv[r, pl.ds(off, L)] * xc for r in range(R))

            accs = lax.fori_loop(0, nch, chunk, tuple(jnp.zeros((L,), f32) for _ in range(R)))
            for half in range(R // L):
                out = jnp.zeros((L,), f32)
                for r in range(L):
                    out = jnp.where(lane == r, jnp.sum(accs[half * L + r]), out)
                h_v[item, pl.ds(half * L, L)] = out

        @pl.loop(0, nblk)
        def _(blk):
            tok0 = wid * tpw + blk * SC_TOK_BLOCK
            it0 = tok0 * G
            pltpu.sync_copy(idx_hbm.at[pl.ds(it0, items)], idx_v)
            pltpu.sync_copy(x_hbm.at[pl.ds(tok0, SC_TOK_BLOCK)], x_v)
            gather(0, 0).start()

            @pl.loop(0, items, step=2)
            def _(it):
                gather(it + 1, 1).start()
                gather(it, 0).wait()
                compute(it, 0)

                @pl.when(it + 2 < items)
                def _():
                    gather(it + 2, 0).start()

                gather(it + 1, 1).wait()
                compute(it + 1, 1)

            pltpu.sync_copy(h_v, h_hbm.at[pl.ds(it0, items)])

    return k(u, idx, x2)


def _pack_rows(t):
    D = t.shape[1]
    b = lax.bitcast_convert_type(t.astype(bf16), jnp.uint16).astype(jnp.uint32)
    return lax.bitcast_convert_type(b[:, :D // 2] | (b[:, D // 2:] << 16), jnp.int32)


def _unpack_words(w):
    return plsc.bitcast(lax.shift_left(w, 16), f32), plsc.bitcast(w & jnp.int32(-65536), f32)


def _sc_gather_wsum(vw, idx, w, N):
    DW = vw.shape[1]
    D = 2 * DW
    G = PEER_PICKS // SC_ROWS
    R, L = SC_ROWS, SC_LANES
    tpw = N // SC_WORKERS
    nblk = tpw // SC_TOK_BLOCK
    items = SC_TOK_BLOCK * G
    nch = DW // L

    @functools.partial(
        pl.kernel, mesh=_sc_mesh(), compiler_params=pltpu.CompilerParams(needs_layout_passes=False),
        out_type=jax.ShapeDtypeStruct((N, D), f32),
        scratch_types=[
            pltpu.VMEM((items, R), jnp.int32),
            pltpu.VMEM((items, R), f32),
            pltpu.VMEM((SC_TOK_BLOCK, D), f32),
            pltpu.VMEM((R, DW), jnp.int32),
            pltpu.VMEM((R, DW), jnp.int32),
            pltpu.SemaphoreType.DMA,
            pltpu.SemaphoreType.DMA,
        ],
        name="peer_sc_wsum",
    )
    def k(v_hbm, idx_hbm, w_hbm, o_hbm, idx_v, w_v, o_v, rows0, rows1, sem0, sem1):
        wid = lax.axis_index("s") * 2 + lax.axis_index("c")
        rows = (rows0, rows1)
        sems = (sem0, sem1)
        lane = lax.iota(jnp.int32, L)

        def gather(item, b):
            return pltpu.make_async_copy(v_hbm.at[idx_v.at[item]], rows[b], sems[b])

        def tree_sum(xs):
            while len(xs) > 1:
                xs = [a + b for a, b in zip(xs[::2], xs[1::2])]
            return xs[0]

        def compute(item, b):
            tl = item // G
            first = (item % G) == 0
            rv = rows[b]
            for half in range(R // L):
                wv = w_v[item, pl.ds(half * L, L)]
                ws = [jnp.sum(jnp.where(lane == r, wv, 0.0)) for r in range(L)]

                @plsc.parallel_loop(0, nch, unroll=2)
                def _(c):
                    off = pl.multiple_of(c * L, L)
                    los, his = [], []
                    for r in range(L):
                        lo, hi = _unpack_words(rv[half * L + r, pl.ds(off, L)])
                        los.append(lo * ws[r])
                        his.append(hi * ws[r])
                    plo = o_v[tl, pl.ds(off, L)]
                    phi = o_v[tl, pl.ds(DW + off, L)]
                    if half == 0:
                        plo = jnp.where(first, 0.0, plo)
                        phi = jnp.where(first, 0.0, phi)
                    o_v[tl, pl.ds(off, L)] = tree_sum(los) + plo
                    o_v[tl, pl.ds(DW + off, L)] = tree_sum(his) + phi

        @pl.loop(0, nblk)
        def _(blk):
            tok0 = wid * tpw + blk * SC_TOK_BLOCK
            it0 = tok0 * G
            pltpu.sync_copy(idx_hbm.at[pl.ds(it0, items)], idx_v)
            pltpu.sync_copy(w_hbm.at[pl.ds(it0, items)], w_v)
            gather(0, 0).start()

            @pl.loop(0, items, step=2)
            def _(it):
                gather(it + 1, 1).start()
                gather(it, 0).wait()
                compute(it, 0)

                @pl.when(it + 2 < items)
                def _():
                    gather(it + 2, 0).start()

                gather(it + 1, 1).wait()
                compute(it + 1, 1)

            pltpu.sync_copy(o_v, o_hbm.at[pl.ds(tok0, SC_TOK_BLOCK)])

    return k(vw, idx, w)


def peer_ffn(x, gain, w_q, keys1, keys2, u, vw):
    B, T, D = x.shape
    N = B * T
    G = PEER_PICKS // SC_ROWS
    xn, eid, gate = _peer_route(x.reshape(N, D), gain, w_q, keys1, keys2)
    eid = eid.reshape(N * G, SC_ROWS)
    gate = gate.reshape(N * G, SC_ROWS)
    h = _sc_gather_dot(u, eid, xn)
    w = gate * jax.nn.gelu(h, approximate=False)
    return _sc_gather_wsum(vw, eid, w, N).reshape(B, T, D)


def kernel(x,
           l0_attn_norm, l0_ret_w_in, l0_ret_out_gain, l0_ret_w_out,
           l0_ffn_norm, l0_peer_w_q, l0_peer_keys1, l0_peer_keys2, l0_peer_u, l0_peer_v,
           l1_attn_norm, l1_gdn_w_in, l1_gdn_conv_w, l1_gdn_a_log, l1_gdn_dt_bias, l1_gdn_out_gain, l1_gdn_w_out,
           l1_ffn_norm, l1_peer_w_q, l1_peer_keys1, l1_peer_keys2, l1_peer_u, l1_peer_v,
           l2_attn_norm, l2_nsa_w_in, l2_nsa_q_gain, l2_nsa_kc_gain, l2_nsa_ks_gain, l2_nsa_kw_gain,
           l2_nsa_pe_k, l2_nsa_w1_k, l2_nsa_w2_k, l2_nsa_pe_v, l2_nsa_w1_v, l2_nsa_w2_v, l2_nsa_w_out,
           l2_ffn_norm, l2_peer_w_q, l2_peer_keys1, l2_peer_keys2, l2_peer_u, l2_peer_v,
           l3_attn_norm, l3_ret_w_in, l3_ret_out_gain, l3_ret_w_out,
           l3_ffn_norm, l3_peer_w_q, l3_peer_keys1, l3_peer_keys2, l3_peer_u, l3_peer_v):
    mixers = (retention_mixer, gated_deltanet_mixer, nsa_mixer)
    attn_norms = (l0_attn_norm, l1_attn_norm, l2_attn_norm, l3_attn_norm)
    mixer_args = (
        (l0_ret_w_in, l0_ret_out_gain, l0_ret_w_out),
        (l1_gdn_w_in, l1_gdn_conv_w, l1_gdn_a_log, l1_gdn_dt_bias, l1_gdn_out_gain, l1_gdn_w_out),
        (l2_nsa_w_in, l2_nsa_q_gain, l2_nsa_kc_gain, l2_nsa_ks_gain, l2_nsa_kw_gain,
         l2_nsa_pe_k, l2_nsa_w1_k, l2_nsa_w2_k, l2_nsa_pe_v, l2_nsa_w1_v, l2_nsa_w2_v, l2_nsa_w_out),
        (l3_ret_w_in, l3_ret_out_gain, l3_ret_w_out),
    )
    ffn_norms = (l0_ffn_norm, l1_ffn_norm, l2_ffn_norm, l3_ffn_norm)
    peer_args = (
        (l0_peer_w_q, l0_peer_keys1, l0_peer_keys2, l0_peer_u, l0_peer_v),
        (l1_peer_w_q, l1_peer_keys1, l1_peer_keys2, l1_peer_u, l1_peer_v),
        (l2_peer_w_q, l2_peer_keys1, l2_peer_keys2, l2_peer_u, l2_peer_v),
        (l3_peer_w_q, l3_peer_keys1, l3_peer_keys2, l3_peer_u, l3_peer_v),
    )
    B = x.shape[0]
    D = x.shape[-1]
    vws = [_pack_rows(p[4]) for p in peer_args]

    def mix_and_route(xc, i):
        xc = xc + mixers[i % MIXER_CYCLE](rms_norm(xc, attn_norms[i]), *mixer_args[i])
        w_q, keys1, keys2, _, _ = peer_args[i]
        xn, eid, gate = _peer_route(xc.reshape(-1, D), ffn_norms[i], w_q, keys1, keys2)
        return xc, (xn, eid.reshape(-1, SC_ROWS), gate.reshape(-1, SC_ROWS))

    chains = [x[c * (B // BATCH_CHAINS):(c + 1) * (B // BATCH_CHAINS)] for c in range(BATCH_CHAINS)]
    staged = [mix_and_route(xc, 0) for xc in chains]
    for i in range(DEPTH):
        u = peer_args[i][3]
        hs = [_sc_gather_dot(u, eid, xn) for _, (xn, eid, _) in staged]
        nxt = []
        for (xc, (xn, eid, gate)), h in zip(staged, hs):
            w = gate * jax.nn.gelu(h, approximate=False)
            xc = xc + _sc_gather_wsum(vws[i], eid, w, xn.shape[0]).reshape(xc.shape)
            nxt.append(mix_and_route(xc, i + 1) if i + 1 < DEPTH else (xc, None))
        staged = nxt
    return jnp.concatenate([xc for xc, _ in staged], axis=0)
```

```python
import functools
import math

import jax
import jax.numpy as jnp
from jax import lax
from jax.experimental import pallas as pl
from jax.experimental.pallas import tpu as pltpu
from jax.experimental.pallas import tpu_sc as plsc

D_MODEL = 1024
DEPTH = 4
f32 = jnp.float32
bf16 = jnp.bfloat16
NORM_EPS = 1e-6
MIXER_CYCLE = 3

RET_HEADS = 8
RET_DK = D_MODEL // RET_HEADS
RET_DV = D_MODEL // RET_HEADS
RET_CHUNK = 128
RET_THETA = 10000.0

GDN_HEADS = 8
GDN_DK = D_MODEL // GDN_HEADS
GDN_DV = D_MODEL // GDN_HEADS
GDN_CONV = 4
GDN_CHUNK = 64

NSA_HEADS = 16
NSA_GROUPS = 4
NSA_HPG = NSA_HEADS // NSA_GROUPS
NSA_DH = D_MODEL // NSA_HEADS
NSA_CMP_BLOCK = 32
NSA_CMP_STRIDE = 16
NSA_CMP_HIDDEN = 256
NSA_SEL_BLOCK = 64
NSA_TOPN = 16
NSA_WINDOW = 512
NSA_QBLOCK = 32
NSA_FORCE = 1e4
ROPE_THETA = 500000.0
ROPE_DIMS = NSA_DH // 4

PEER_NKEYS = 128
PEER_EXPERTS = PEER_NKEYS * PEER_NKEYS
PEER_HEADS = 8
PEER_QDIM = 256
PEER_TOPK = 16
PEER_TOKEN_BLOCK = 128


def _mm_kernel(a_ref, b_ref, o_ref):
    o_ref[...] = jnp.dot(a_ref[...].astype(bf16), b_ref[...].astype(bf16),
                         preferred_element_type=f32)


def _mm(a, b, tm=512, tn=1024):
    M, K = a.shape
    _, N = b.shape
    tn = min(tn, N)
    if N % tn:
        tn = N
    return pl.pallas_call(
        _mm_kernel,
        grid=(M // tm, N // tn),
        in_specs=[pl.BlockSpec((tm, K), lambda i, j: (i, 0)),
                  pl.BlockSpec((K, tn), lambda i, j: (0, j))],
        out_specs=pl.BlockSpec((tm, tn), lambda i, j: (i, j)),
        out_shape=jax.ShapeDtypeStruct((M, N), f32),
        compiler_params=pltpu.CompilerParams(
            dimension_semantics=("parallel", "parallel"),
            vmem_limit_bytes=48 * 1024 * 1024),
    )(a, b)


def _proj(h, w):
    B, T, D = h.shape
    return _mm(h.reshape(B * T, D), w).reshape(B, T, -1)


def rms_norm(x, gain):
    xf = x.astype(f32)
    y = xf * lax.rsqrt(jnp.mean(xf * xf, axis=-1, keepdims=True) + NORM_EPS)
    return (y * gain.astype(f32)).astype(x.dtype)


def l2_norm(x):
    xf = x.astype(f32)
    return xf * lax.rsqrt(jnp.sum(xf * xf, axis=-1, keepdims=True) + NORM_EPS)


def rotary(x, pos, rot_dims, theta):
    half = rot_dims // 2
    inv_freq = jnp.power(theta, -jnp.arange(half, dtype=f32) / half)
    ang = pos.astype(f32)[:, None] * inv_freq[None, :]
    cos, sin = jnp.cos(ang), jnp.sin(ang)
    xf = x.astype(f32)
    x1, x2 = xf[..., :half], xf[..., half:rot_dims]
    out = jnp.concatenate([x1 * cos - x2 * sin, x2 * cos + x1 * sin, xf[..., rot_dims:]], axis=-1)
    return out.astype(x.dtype)


def masked_softmax(s, mask):
    s = jnp.where(mask, s.astype(f32), -1e30)
    m = jnp.max(s, axis=-1, keepdims=True)
    e = jnp.where(mask, jnp.exp(s - m), 0.0)
    return e / jnp.maximum(jnp.sum(e, axis=-1, keepdims=True), 1e-30)


def to_heads(z, n_heads):
    B, T, _ = z.shape
    return z.reshape(B, T, n_heads, -1).transpose(0, 2, 1, 3)


def causal_depthwise_conv(x, w):
    K, C = w.shape
    return lax.conv_general_dilated(x, w.astype(x.dtype)[:, None, :], window_strides=(1,),
                                    padding=[(K - 1, 0)], dimension_numbers=('NWC', 'WIO', 'NWC'),
                                    feature_group_count=C)


def rotary_bthd(x, pos, rot_dims, theta):
    half = rot_dims // 2
    inv_freq = jnp.power(theta, -jnp.arange(half, dtype=f32) / half)
    ang = pos.astype(f32)[:, None] * inv_freq[None, :]
    cos, sin = jnp.cos(ang)[:, None, :], jnp.sin(ang)[:, None, :]
    xf = x.astype(f32)
    x1, x2 = xf[..., :half], xf[..., half:rot_dims]
    out = jnp.concatenate([x1 * cos - x2 * sin, x2 * cos + x1 * sin, xf[..., rot_dims:]], axis=-1)
    return out.astype(x.dtype)


def _ret_kernel(q_ref, k_ref, v_ref, intra_ref, qd_ref, kd_ref, cd_ref, o_ref, state):
    H, dk, dv = RET_HEADS, RET_DK, RET_DV

    @pl.when(pl.program_id(1) == 0)
    def _():
        state[...] = jnp.zeros_like(state)

    dn_t = (((1,), (1,)), ((), ()))
    dn_k = (((0,), (0,)), ((), ()))
    for h in range(H):
        q = q_ref[:, h * dk:(h + 1) * dk].astype(bf16)
        k = k_ref[:, h * dk:(h + 1) * dk]
        v = v_ref[:, h * dv:(h + 1) * dv].astype(bf16)
        st = state[h]
        s = lax.dot_general(q, k.astype(bf16), dn_t, preferred_element_type=f32) * intra_ref[h]
        o_ref[:, h * dv:(h + 1) * dv] = (
            jnp.dot(s.astype(bf16), v, preferred_element_type=f32)
            + jnp.dot(q, st.astype(bf16), preferred_element_type=f32) * qd_ref[h])
        kk = (k * kd_ref[h]).astype(bf16)
        state[h] = st * cd_ref[h] + lax.dot_general(kk, v, dn_k, preferred_element_type=f32)


def retention_core(q, k, v):
    B, T, _ = q.shape
    H, dk, dv, C = RET_HEADS, RET_DK, RET_DV, RET_CHUNK
    log_gamma = jnp.log1p(-jnp.power(2.0, -5.0 - jnp.arange(H, dtype=f32)))
    idx = jnp.arange(C, dtype=f32)
    rel = idx[:, None] - idx[None, :]
    intra = jnp.where(rel >= 0, jnp.exp(log_gamma[:, None, None] * jnp.maximum(rel, 0.0)), 0.0)
    q_decay = jnp.broadcast_to(jnp.exp(log_gamma[:, None] * (idx + 1.0))[:, :, None], (H, C, dv))
    k_decay = jnp.broadcast_to(jnp.exp(log_gamma[:, None] * (C - 1.0 - idx))[:, :, None], (H, C, dk))
    chunk_decay = jnp.broadcast_to(jnp.exp(log_gamma * C)[:, None, None], (H, 1, dv))

    def blk(w):
        return pl.BlockSpec((None, C, w), lambda b, c: (b, c, 0))

    def const(a):
        return pl.BlockSpec(a.shape, lambda b, c: (0, 0, 0))

    return pl.pallas_call(
        _ret_kernel,
        grid=(B, T // C),
        in_specs=[blk(H * dk), blk(H * dk), blk(H * dv),
                  const(intra), const(q_decay), const(k_decay), const(chunk_decay)],
        out_specs=blk(H * dv),
        out_shape=jax.ShapeDtypeStruct((B, T, H * dv), f32),
        scratch_shapes=[pltpu.VMEM((H, dk, dv), f32)],
        compiler_params=pltpu.CompilerParams(dimension_semantics=("parallel", "arbitrary")),
        name="retention_core",
    )(q, k, v, intra, q_decay, k_decay, chunk_decay)


def retention_mixer(h, w_in, out_gain, w_out):
    B, T, D = h.shape
    H, dk, dv = RET_HEADS, RET_DK, RET_DV
    q, k, v, g = jnp.split(_proj(h, w_in), 4, axis=-1)
    pos = jnp.arange(T)
    q = rotary_bthd(q.reshape(B, T, H, dk), pos, dk, RET_THETA).reshape(B, T, H * dk)
    k = (rotary_bthd(k.reshape(B, T, H, dk), pos, dk, RET_THETA) * (dk ** -0.5)).reshape(B, T, H * dk)
    o = retention_core(q, k, v).reshape(B, T, H, dv)
    o = rms_norm(o, out_gain.reshape(H, dv)).reshape(B, T, H * dv).astype(h.dtype)
    return _proj(jax.nn.silu(g) * o, w_out)


def _gdn_kernel(q_ref, k_ref, v_ref, gc_ref, gr_ref, b_ref, o_ref, state):
    H, dk, dv, C = GDN_HEADS, GDN_DK, GDN_DV, GDN_CHUNK

    @pl.when(pl.program_id(1) == 0)
    def _():
        state[...] = jnp.zeros_like(state)

    dn_t = (((1,), (1,)), ((), ()))
    dn_k = (((0,), (0,)), ((), ()))
    hi = lax.Precision.HIGHEST
    row = lax.broadcasted_iota(jnp.int32, (C, C), 0)
    col = lax.broadcasted_iota(jnp.int32, (C, C), 1)
    eye = (row == col).astype(f32)

    def mm(a, b):
        return jnp.dot(a.astype(bf16), b.astype(bf16), preferred_element_type=f32)

    for h in range(H):
        q = q_ref[:, h * dk:(h + 1) * dk]
        k = k_ref[:, h * dk:(h + 1) * dk]
        v = v_ref[:, h * dv:(h + 1) * dv]
        gc = gc_ref[:, h:h + 1]
        gr = gr_ref[h:h + 1, :]
        beta = b_ref[:, h:h + 1]
        g_last = gr[:, C - 1:C]
        S = state[h]
        decay = jnp.where(row >= col, jnp.exp(jnp.minimum(gc - gr, 0.0)), 0.0)
        k_beta = k * beta
        kk = lax.dot_general(k_beta.astype(bf16), k.astype(bf16), dn_t, preferred_element_type=f32)
        lower = jnp.where(row > col, kk * decay, 0.0)
        P = -lower
        T = eye + P
        n = 1
        while n * 2 < C:
            P = jnp.dot(P, P, precision=hi, preferred_element_type=f32)
            T = T + jnp.dot(T, P, precision=hi, preferred_element_type=f32)
            n *= 2
        u = mm(T, v * beta)
        w = mm(T, k_beta * jnp.exp(gc))
        qk = lax.dot_general(q.astype(bf16), k.astype(bf16), dn_t, preferred_element_type=f32) * decay
        v_new = u - mm(w, S)
        o_ref[:, h * dv:(h + 1) * dv] = mm(q * jnp.exp(gc), S) + mm(qk, v_new)
        k_dec = (k * jnp.exp(g_last - gc)).astype(bf16)
        state[h] = S * jnp.exp(g_last) + lax.dot_general(k_dec, v_new.astype(bf16), dn_k, preferred_element_type=f32)


def gdn_core(q, k, v, g_cum, beta):
    B, T, _ = q.shape
    H, dk, dv, C = GDN_HEADS, GDN_DK, GDN_DV, GDN_CHUNK
    n = T // C
    g_row = g_cum.reshape(B, n, C, H).transpose(0, 1, 3, 2)

    def blk(w):
        return pl.BlockSpec((None, C, w), lambda b, c: (b, c, 0))

    return pl.pallas_call(
        _gdn_kernel,
        grid=(B, n),
        in_specs=[blk(H * dk), blk(H * dk), blk(H * dv), blk(H),
                  pl.BlockSpec((None, None, H, C), lambda b, c: (b, c, 0, 0)), blk(H)],
        out_specs=blk(H * dv),
        out_shape=jax.ShapeDtypeStruct((B, T, H * dv), f32),
        scratch_shapes=[pltpu.VMEM((H, dk, dv), f32)],
        compiler_params=pltpu.CompilerParams(dimension_semantics=("parallel", "arbitrary")),
        name="gdn_core",
    )(q, k, v, g_cum, g_row, beta)


def gated_deltanet_mixer(h, w_in, conv_w, a_log, dt_bias, out_gain, w_out):
    B, T, D = h.shape
    H, dk, dv, C = GDN_HEADS, GDN_DK, GDN_DV, GDN_CHUNK
    n_qk, n_v = H * dk, H * dv
    n_conv = 2 * n_qk + n_v
    n = T // C
    proj = jnp.dot(h, w_in)
    qkv = jax.nn.silu(causal_depthwise_conv(proj[..., :n_conv], conv_w))
    z = proj[..., n_conv:n_conv + n_v]
    beta = jax.nn.sigmoid(proj[..., n_conv + n_v:n_conv + n_v + H].astype(f32))
    a = proj[..., n_conv + n_v + H:].astype(f32)
    g = -jnp.exp(a_log.astype(f32)) * jax.nn.softplus(a + dt_bias.astype(f32))

    g_cum = jnp.cumsum(g.reshape(B, n, C, H), axis=2).reshape(B, T, H)
    q = (l2_norm(qkv[..., :n_qk].reshape(B, T, H, dk)) * (dk ** -0.5)).reshape(B, T, n_qk)
    k = l2_norm(qkv[..., n_qk:2 * n_qk].reshape(B, T, H, dk)).reshape(B, T, n_qk)
    v = qkv[..., 2 * n_qk:].astype(f32)
    o = gdn_core(q, k, v, g_cum, beta).reshape(B, T, H, dv)
    o = rms_norm(o, out_gain) * jax.nn.silu(z.reshape(B, T, H, dv).astype(f32))
    return _proj(o.reshape(B, T, H * dv).astype(h.dtype), w_out)


def compress_blocks(zz, idx, pe, w1, w2):
    Bz, Gz, Tz, dz = zz.shape
    assert NSA_CMP_BLOCK == 2 * NSA_CMP_STRIDE and Tz % NSA_CMP_STRIDE == 0
    pieces = zz.reshape(Bz, Gz, Tz // NSA_CMP_STRIDE, NSA_CMP_STRIDE, dz)
    blocks = jnp.concatenate([pieces[:, :, :-1], pieces[:, :, 1:]], axis=3) + pe
    flat = blocks.reshape(blocks.shape[0], blocks.shape[1], blocks.shape[2], -1)
    return jax.nn.gelu(flat @ w1, approximate=False) @ w2


NSA_TQ = 128
NSA_TK = 512
LANE = 128
NEG = -1e30


def _nsa_kernel(q_ref, kc_ref, vc_ref, ks_ref, vs_ref, kw_ref, vw_ref, g_ref, ovl_ref, exp_ref, o_ref, *, top_n):
    Hg, TQ, dh = q_ref.shape
    TK, W = NSA_TK, NSA_WINDOW
    ncp = kc_ref.shape[0]
    nsp = exp_ref.shape[0]
    t0 = pl.program_id(2) * TQ
    q2 = q_ref[...].reshape(Hg * TQ, dh)
    tpos = t0 + lax.broadcasted_iota(jnp.int32, (TQ, 1), 0)
    dn = (((1,), (1,)), ((), ()))

    def softmax_rows(s, mask):
        s3 = jnp.where(mask[None], s.reshape(Hg, TQ, -1), NEG)
        m = jnp.max(s3, axis=-1, keepdims=True)
        e = jnp.where(mask[None], jnp.exp(s3 - m), 0.0)
        return e / jnp.maximum(jnp.sum(e, axis=-1, keepdims=True), 1e-30)

    s = lax.dot_general(q2, kc_ref[...], dn, preferred_element_type=f32)
    cend = lax.broadcasted_iota(jnp.int32, (1, ncp), 1) * NSA_CMP_STRIDE + (NSA_CMP_BLOCK - 1)
    p = softmax_rows(s, cend <= tpos)
    o_cmp = jnp.dot(p.reshape(Hg * TQ, ncp).astype(bf16), vc_ref[...], preferred_element_type=f32)

    p_all = jnp.concatenate([p[h] for h in range(Hg)], axis=-1).astype(bf16)
    imp = jnp.dot(p_all, ovl_ref[...], preferred_element_type=f32)
    blk = lax.broadcasted_iota(jnp.int32, (1, nsp), 1)
    cur = tpos // NSA_SEL_BLOCK
    forced = (blk == 0) | (blk == cur) | (blk == cur - 1)
    score = jnp.where(forced, NSA_FORCE, jnp.where(blk <= cur, imp, -NSA_FORCE))
    n_real = exp_ref.shape[1] // NSA_SEL_BLOCK
    score = jnp.where(blk < n_real, score, -jnp.inf)
    st = score.T
    rows = lax.broadcasted_iota(jnp.int32, st.shape, 0)
    sel_t = jnp.zeros(st.shape, f32)
    for _ in range(top_n):
        m = jnp.max(st, axis=0, keepdims=True)
        first = jnp.min(jnp.where(st == m, rows, nsp), axis=0, keepdims=True)
        hit = rows == first
        sel_t = jnp.where(hit, 1.0, sel_t)
        st = jnp.where(hit, -jnp.inf, st)
    sel = sel_t.T.astype(bf16)

    n_tiles = (t0 + TQ + TK - 1) // TK

    def sel_tile(kt, carry):
        m, l, acc = carry
        koff = pl.multiple_of(kt * TK, TK)
        k_t = ks_ref[pl.ds(koff, TK), :]
        v_t = vs_ref[pl.ds(koff, TK), :]
        s = lax.dot_general(q2, k_t, dn, preferred_element_type=f32)
        picked = jnp.dot(sel, exp_ref[:, pl.ds(koff, TK)], preferred_element_type=f32)
        kpos = koff + lax.broadcasted_iota(jnp.int32, (1, TK), 1)
        mask = ((picked > 0.5) & (kpos <= tpos))[None]
        s3 = jnp.where(mask, s.reshape(Hg, TQ, TK), NEG)
        m_new = jnp.maximum(m, jnp.max(s3, axis=-1, keepdims=True))
        alpha = jnp.exp(m - m_new)
        e = jnp.where(mask, jnp.exp(s3 - m_new), 0.0)
        l = l * alpha + jnp.sum(e, axis=-1, keepdims=True)
        pv = jnp.dot(e.reshape(Hg * TQ, TK).astype(bf16), v_t, preferred_element_type=f32)
        return m_new, l, acc * alpha + pv.reshape(Hg, TQ, dh)

    m0 = jnp.full((Hg, TQ, 1), NEG, f32)
    l0 = jnp.zeros((Hg, TQ, 1), f32)
    a0 = jnp.zeros((Hg, TQ, dh), f32)
    _, l, acc = lax.fori_loop(0, n_tiles, sel_tile, (m0, l0, a0))
    o_slc = acc / jnp.maximum(l, 1e-30)

    woff = pl.multiple_of(t0, TQ)
    k_w = kw_ref[pl.ds(woff, W + TQ), :]
    v_w = vw_ref[pl.ds(woff, W + TQ), :]
    s = lax.dot_general(q2, k_w, dn, preferred_element_type=f32)
    wpos = t0 - W + lax.broadcasted_iota(jnp.int32, (1, W + TQ), 1)
    dpos = tpos - wpos
    p = softmax_rows(s, (dpos >= 0) & (dpos < W) & (wpos >= 0))
    o_win = jnp.dot(p.reshape(Hg * TQ, W + TQ).astype(bf16), v_w, preferred_element_type=f32)

    g = g_ref[...]
    o_cmp = o_cmp.reshape(Hg, TQ, dh)
    o_win = o_win.reshape(Hg, TQ, dh)
    for h in range(Hg):
        o_ref[:, h * dh:(h + 1) * dh] = (g[:, h:h + 1] * o_cmp[h] + g[:, Hg + h:Hg + h + 1] * o_slc[h]
                                         + g[:, 2 * Hg + h:2 * Hg + h + 1] * o_win[h])


def nsa_attention(q, k_cmp, v_cmp, ks, vs, kw, vw, gates):
    B, G, Hg, T, dh = q.shape
    L, S_, SEL, W, TQ = NSA_CMP_BLOCK, NSA_CMP_STRIDE, NSA_SEL_BLOCK, NSA_WINDOW, NSA_TQ
    n_cmp = k_cmp.shape[2]
    n_sel = T // SEL
    top_n = min(NSA_TOPN, n_sel)
    ncp = -(-n_cmp // LANE) * LANE
    nsp = -(-n_sel // LANE) * LANE
    cmp_start = jnp.arange(n_cmp) * S_
    sel_start = jnp.arange(n_sel) * SEL
    overlap = jnp.clip(jnp.minimum(cmp_start[:, None] + L, sel_start[None, :] + SEL)
                       - jnp.maximum(cmp_start[:, None], sel_start[None, :]), 0).astype(f32) / L
    ovl = jnp.pad(overlap, ((0, ncp - n_cmp), (0, nsp - n_sel)))
    ovl = jnp.tile(ovl, (Hg, 1)).astype(bf16)
    expand = (jnp.arange(nsp)[:, None] == (jnp.arange(T)[None, :] // SEL)).astype(bf16)
    padc = ((0, 0), (0, 0), (0, ncp - n_cmp), (0, 0))
    padw = ((0, 0), (0, 0), (W, 0), (0, 0))
    g2 = gates.transpose(0, 2, 4, 1, 3).reshape(B, G, T, 3 * Hg)
    g2 = jnp.pad(g2, ((0, 0), (0, 0), (0, 0), (0, 16 - 3 * Hg)))

    def kv(n):
        return pl.BlockSpec((None, None, n, dh), lambda b, g, i: (b, g, 0, 0))

    return pl.pallas_call(
        functools.partial(_nsa_kernel, top_n=top_n),
        grid=(B, G, T // TQ),
        in_specs=[pl.BlockSpec((None, None, Hg, TQ, dh), lambda b, g, i: (b, g, 0, i, 0)),
                  kv(ncp), kv(ncp), kv(T), kv(T), kv(T + W), kv(T + W),
                  pl.BlockSpec((None, None, TQ, 16), lambda b, g, i: (b, g, i, 0)),
                  pl.BlockSpec((Hg * ncp, nsp), lambda b, g, i: (0, 0)),
                  pl.BlockSpec((nsp, T), lambda b, g, i: (0, 0))],
        out_specs=pl.BlockSpec((None, TQ, Hg * dh), lambda b, g, i: (b, i, g)),
        out_shape=jax.ShapeDtypeStruct((B, T, G * Hg * dh), f32),
        compiler_params=pltpu.CompilerParams(
            dimension_semantics=("parallel", "parallel", "arbitrary"), vmem_limit_bytes=56 * 1024 * 1024),
        name="nsa_attention",
    )(q.astype(bf16), jnp.pad(k_cmp, padc).astype(bf16), jnp.pad(v_cmp, padc).astype(bf16),
      ks.astype(bf16), vs.astype(bf16), jnp.pad(kw, padw).astype(bf16), jnp.pad(vw, padw).astype(bf16),
      g2, ovl, expand)


def nsa_mixer(h, w_in, q_gain, kc_gain, ks_gain, kw_gain, pe_k, w1_k, w2_k, pe_v, w1_v, w2_v, w_out):
    B, T, D = h.shape
    H, G, Hg, dh = NSA_HEADS, NSA_GROUPS, NSA_HPG, NSA_DH
    kvw = G * dh
    L, S_ = NSA_CMP_BLOCK, NSA_CMP_STRIDE
    n_cmp = (T - L) // S_ + 1

    proj = _proj(h, w_in)
    q = proj[..., :H * dh].reshape(B, T, H, dh)
    kc, vc, ks, vs, kw, vw = [proj[..., H * dh + j * kvw:H * dh + (j + 1) * kvw]
                              .reshape(B, T, G, dh).transpose(0, 2, 1, 3) for j in range(6)]
    gates = jax.nn.sigmoid(proj[..., H * dh + 6 * kvw:].astype(f32))
    gates = gates.reshape(B, T, 3, H).transpose(0, 2, 3, 1).reshape(B, 3, G, Hg, T)

    pos = jnp.arange(T)
    q = rotary(rms_norm(q, q_gain).transpose(0, 2, 1, 3), pos, ROPE_DIMS, ROPE_THETA)
    q = q.reshape(B, G, Hg, T, dh) * (dh ** -0.5)
    ks = rotary(rms_norm(ks, ks_gain), pos, ROPE_DIMS, ROPE_THETA)
    kw = rotary(rms_norm(kw, kw_gain), pos, ROPE_DIMS, ROPE_THETA)

    cmp_start = jnp.arange(n_cmp) * S_
    cmp_end = cmp_start + L - 1
    cmp_idx = cmp_start[:, None] + jnp.arange(L)[None, :]
    k_cmp = rotary(rms_norm(compress_blocks(kc, cmp_idx, pe_k, w1_k, w2_k), kc_gain), cmp_end,
                   ROPE_DIMS, ROPE_THETA)
    v_cmp = compress_blocks(vc, cmp_idx, pe_v, w1_v, w2_v)

    o = nsa_attention(q, k_cmp, v_cmp, ks, vs, kw, vw, gates)
    return _proj(o, w_out)


PEER_TM = 128
SC_LANES = 16
SC_WORKERS = 32
SC_ROWS = 32
SC_TOK_BLOCK = 8
PEER_PICKS = PEER_HEADS * PEER_TOPK
BATCH_CHAINS = 2


def _top_rows(s, k, order=None, payload=None):
    if order is None:
        order = lax.broadcasted_iota(jnp.int32, s.shape, 0)
    big = jnp.int32(2 ** 30)
    vals, ids = [], []
    for _ in range(k):
        m = jnp.max(s, axis=0, keepdims=True)
        first = jnp.min(jnp.where(s == m, order, big), axis=0, keepdims=True)
        hit = order == first
        vals.append(m)
        if payload is None:
            ids.append(first)
        else:
            ids.append(jnp.max(jnp.where(hit, payload, -1), axis=0, keepdims=True))
        s = jnp.where(hit, -jnp.inf, s)
    return jnp.concatenate(vals, axis=0), jnp.concatenate(ids, axis=0)


def _pair_candidates(v1, i1, v2, i2):
    K, S = PEER_TOPK, 8
    rows8 = lax.broadcasted_iota(jnp.int32, (S, v1.shape[1]), 0)
    vals = [v1[i:i + 1, :] + v2[0:S, :] for i in range(S)]
    flat = [i * K + rows8 for i in range(S)]
    cid = [i1[i:i + 1, :] * PEER_NKEYS + i2[0:S, :] for i in range(S)]
    vals.append(v1[0:1, :] + v2[S:K, :])
    flat.append(S + rows8)
    cid.append(i1[0:1, :] * PEER_NKEYS + i2[S:K, :])
    vals.append(v1[S:K, :] + v2[0:1, :])
    flat.append((S + rows8) * K)
    cid.append(i1[S:K, :] * PEER_NKEYS + i2[0:1, :])
    return tuple(jnp.concatenate(xs, axis=0) for xs in (vals, flat, cid))


def _peer_route_kernel(x_ref, g_ref, wq_ref, k1_ref, k2_ref, xn_ref, eid_ref, gate_ref, eid_t, gate_t):
    x = x_ref[...]
    xn = x * lax.rsqrt(jnp.mean(x * x, axis=-1, keepdims=True) + NORM_EPS) * g_ref[...]
    xn_ref[...] = xn
    q = jnp.dot(xn.astype(bf16), wq_ref[...], preferred_element_type=f32)
    half = PEER_QDIM // 2
    K = PEER_TOPK
    k1 = k1_ref[...]
    k2 = k2_ref[...]
    dn = (((1,), (1,)), ((), ()))
    for h in range(PEER_HEADS):
        q1 = q[:, h * PEER_QDIM:h * PEER_QDIM + half].astype(bf16)
        q2 = q[:, h * PEER_QDIM + half:(h + 1) * PEER_QDIM].astype(bf16)
        s1 = lax.dot_general(k1, q1, dn, preferred_element_type=f32)
        s2 = lax.dot_general(k2, q2, dn, preferred_element_type=f32)
        v1, i1 = _top_rows(s1, K)
        v2, i2 = _top_rows(s2, K)
        cand, flat, cid = _pair_candidates(v1, i1, v2, i2)
        sc, eid = _top_rows(cand, K, order=flat, payload=cid)
        e = jnp.exp(sc - jnp.max(sc, axis=0, keepdims=True))
        eid_t[h * K:(h + 1) * K, :] = eid
        gate_t[h * K:(h + 1) * K, :] = e / jnp.sum(e, axis=0, keepdims=True)
    eid_ref[...] = eid_t[...].T
    gate_ref[...] = gate_t[...].T


def _peer_route(x2, gain, w_q, keys1, keys2):
    N, D = x2.shape
    nq = PEER_HEADS * PEER_QDIM
    P = PEER_PICKS
    tm = PEER_TM
    return pl.pallas_call(
        _peer_route_kernel,
        grid=(N // tm,),
        in_specs=[pl.BlockSpec((tm, D), lambda i: (i, 0)),
                  pl.BlockSpec((1, D), lambda i: (0, 0)),
                  pl.BlockSpec((D, nq), lambda i: (0, 0)),
                  pl.BlockSpec(keys1.shape, lambda i: (0, 0)),
                  pl.BlockSpec(keys2.shape, lambda i: (0, 0))],
        out_specs=[pl.BlockSpec((tm, D), lambda i: (i, 0)),
                   pl.BlockSpec((tm, P), lambda i: (i, 0)),
                   pl.BlockSpec((tm, P), lambda i: (i, 0))],
        out_shape=[jax.ShapeDtypeStruct((N, D), f32),
                   jax.ShapeDtypeStruct((N, P), jnp.int32),
                   jax.ShapeDtypeStruct((N, P), f32)],
        scratch_shapes=[pltpu.VMEM((P, tm), jnp.int32), pltpu.VMEM((P, tm), f32)],
        compiler_params=pltpu.CompilerParams(
            dimension_semantics=("parallel",), vmem_limit_bytes=48 * 1024 * 1024),
        name="peer_route",
    )(x2, gain.reshape(1, D), w_q.astype(bf16), keys1.astype(bf16), keys2.astype(bf16))


_SC_PARAMS = pltpu.CompilerParams(needs_layout_passes=False, use_tc_tiling_on_sc=True)


def _sc_mesh():
    return plsc.VectorSubcoreMesh(core_axis_name="c", subcore_axis_name="s")


def _sc_gather_dot(u, idx, x2):
    N, D = x2.shape
    G = PEER_PICKS // SC_ROWS
    R, L = SC_ROWS, SC_LANES
    tpw = N // SC_WORKERS
    nblk = tpw // SC_TOK_BLOCK
    items = SC_TOK_BLOCK * G
    nch = D // L

    @functools.partial(
        pl.kernel, mesh=_sc_mesh(), compiler_params=_SC_PARAMS,
        out_type=jax.ShapeDtypeStruct((N * G, R), f32),
        scratch_types=[
            pltpu.VMEM((items, R), jnp.int32),
            pltpu.VMEM((SC_TOK_BLOCK, D), f32),
            pltpu.VMEM((items, R), f32),
            pltpu.VMEM((R, D), f32),
            pltpu.VMEM((R, D), f32),
            pltpu.SemaphoreType.DMA,
            pltpu.SemaphoreType.DMA,
        ],
        name="peer_sc_dot",
    )
    def k(u_hbm, idx_hbm, x_hbm, h_hbm, idx_v, x_v, h_v, rows0, rows1, sem0, sem1):
        wid = lax.axis_index("s") * 2 + lax.axis_index("c")
        rows = (rows0, rows1)
        sems = (sem0, sem1)
        lane = lax.iota(jnp.int32, L)

        def gather(item, b):
            return pltpu.make_async_copy(u_hbm.at[idx_v.at[item]], rows[b], sems[b])

        def compute(item, b):
            tl = item // G
            rv = rows[b]

            def chunk(c, accs):
                off = pl.multiple_of(c * L, L)
                xc = x_v[tl, pl.ds(off, L)]
                return tuple(accs[r] + rv[r, pl.ds(off, L)] * xc for r in range(R))

            accs = lax.fori_loop(0, nch, chunk, tuple(jnp.zeros((L,), f32) for _ in range(R)))
            for half in range(R // L):
                out = jnp.zeros((L,), f32)
                for r in range(L):
                    out = jnp.where(lane == r, jnp.sum(accs[half * L + r]), out)
                h_v[item, pl.ds(half * L, L)] = out

        @pl.loop(0, nblk)
        def _(blk):
            tok0 = wid * tpw + blk * SC_TOK_BLOCK
            it0 = tok0 * G
            pltpu.sync_copy(idx_hbm.at[pl.ds(it0, items)], idx_v)
            pltpu.sync_copy(x_hbm.at[pl.ds(tok0, SC_TOK_BLOCK)], x_v)
            gather(0, 0).start()

            @pl.loop(0, items, step=2)
            def _(it):
                gather(it + 1, 1).start()
                gather(it, 0).wait()
                compute(it, 0)

                @pl.when(it + 2 < items)
                def _():
                    gather(it + 2, 0).start()

                gather(it + 1, 1).wait()
                compute(it + 1, 1)

            pltpu.sync_copy(h_v, h_hbm.at[pl.ds(it0, items)])

    return k(u, idx, x2)


def _pack_rows(t):
    D = t.shape[1]
    b = lax.bitcast_convert_type(t.astype(bf16), jnp.uint16).astype(jnp.uint32)
    return lax.bitcast_convert_type(b[:, :D // 2] | (b[:, D // 2:] << 16), jnp.int32)


def _unpack_words(w):
    return plsc.bitcast(lax.shift_left(w, 16), f32), plsc.bitcast(w & jnp.int32(-65536), f32)


def _sc_gather_wsum(vw, idx, w, N):
    DW = vw.shape[1]
    D = 2 * DW
    G = PEER_PICKS // SC_ROWS
    R, L = SC_ROWS, SC_LANES
    tpw = N // SC_WORKERS
    nblk = tpw // SC_TOK_BLOCK
    items = SC_TOK_BLOCK * G
    nch = DW // L

    @functools.partial(
        pl.kernel, mesh=_sc_mesh(), compiler_params=_SC_PARAMS,
        out_type=jax.ShapeDtypeStruct((N, D), f32),
        scratch_types=[
            pltpu.VMEM((items, R), jnp.int32),
            pltpu.VMEM((items, R), f32),
            pltpu.VMEM((SC_TOK_BLOCK, D), f32),
            pltpu.VMEM((R, DW), jnp.int32),
            pltpu.VMEM((R, DW), jnp.int32),
            pltpu.SemaphoreType.DMA,
            pltpu.SemaphoreType.DMA,
        ],
        name="peer_sc_wsum",
    )
    def k(v_hbm, idx_hbm, w_hbm, o_hbm, idx_v, w_v, o_v, rows0, rows1, sem0, sem1):
        wid = lax.axis_index("s") * 2 + lax.axis_index("c")
        rows = (rows0, rows1)
        sems = (sem0, sem1)
        lane = lax.iota(jnp.int32, L)

        def gather(item, b):
            return pltpu.make_async_copy(v_hbm.at[idx_v.at[item]], rows[b], sems[b])

        def tree_sum(xs):
            while len(xs) > 1:
                xs = [a + b for a, b in zip(xs[::2], xs[1::2])]
            return xs[0]

        def compute(item, b):
            tl = item // G
            first = (item % G) == 0
            rv = rows[b]
            for half in range(R // L):
                wv = w_v[item, pl.ds(half * L, L)]
                ws = [jnp.sum(jnp.where(lane == r, wv, 0.0)) for r in range(L)]

                @plsc.parallel_loop(0, nch, unroll=2)
                def _(c):
                    off = pl.multiple_of(c * L, L)
                    los, his = [], []
                    for r in range(L):
                        lo, hi = _unpack_words(rv[half * L + r, pl.ds(off, L)])
                        los.append(lo * ws[r])
                        his.append(hi * ws[r])
                    plo = o_v[tl, pl.ds(off, L)]
                    phi = o_v[tl, pl.ds(DW + off, L)]
                    if half == 0:
                        plo = jnp.where(first, 0.0, plo)
                        phi = jnp.where(first, 0.0, phi)
                    o_v[tl, pl.ds(off, L)] = tree_sum(los) + plo
                    o_v[tl, pl.ds(DW + off, L)] = tree_sum(his) + phi

        @pl.loop(0, nblk)
        def _(blk):
            tok0 = wid * tpw + blk * SC_TOK_BLOCK
            it0 = tok0 * G
            pltpu.sync_copy(idx_hbm.at[pl.ds(it0, items)], idx_v)
            pltpu.sync_copy(w_hbm.at[pl.ds(it0, items)], w_v)
            gather(0, 0).start()

            @pl.loop(0, items, step=2)
            def _(it):
                gather(it + 1, 1).start()
                gather(it, 0).wait()
                compute(it, 0)

                @pl.when(it + 2 < items)
                def _():
                    gather(it + 2, 0).start()

                gather(it + 1, 1).wait()
                compute(it + 1, 1)

            pltpu.sync_copy(o_v, o_hbm.at[pl.ds(tok0, SC_TOK_BLOCK)])

    return k(vw, idx, w)


def peer_ffn(x, gain, w_q, keys1, keys2, u, vw):
    B, T, D = x.shape
    N = B * T
    G = PEER_PICKS // SC_ROWS
    xn, eid, gate = _peer_route(x.reshape(N, D), gain, w_q, keys1, keys2)
    eid = eid.reshape(N * G, SC_ROWS)
    gate = gate.reshape(N * G, SC_ROWS)
    h = _sc_gather_dot(u, eid, xn)
    w = gate * jax.nn.gelu(h, approximate=False)
    return _sc_gather_wsum(vw, eid, w, N).reshape(B, T, D)


def kernel(x,
           l0_attn_norm, l0_ret_w_in, l0_ret_out_gain, l0_ret_w_out,
           l0_ffn_norm, l0_peer_w_q, l0_peer_keys1, l0_peer_keys2, l0_peer_u, l0_peer_v,
           l1_attn_norm, l1_gdn_w_in, l1_gdn_conv_w, l1_gdn_a_log, l1_gdn_dt_bias, l1_gdn_out_gain, l1_gdn_w_out,
           l1_ffn_norm, l1_peer_w_q, l1_peer_keys1, l1_peer_keys2, l1_peer_u, l1_peer_v,
           l2_attn_norm, l2_nsa_w_in, l2_nsa_q_gain, l2_nsa_kc_gain, l2_nsa_ks_gain, l2_nsa_kw_gain,
           l2_nsa_pe_k, l2_nsa_w1_k, l2_nsa_w2_k, l2_nsa_pe_v, l2_nsa_w1_v, l2_nsa_w2_v, l2_nsa_w_out,
           l2_ffn_norm, l2_peer_w_q, l2_peer_keys1, l2_peer_keys2, l2_peer_u, l2_peer_v,
           l3_attn_norm, l3_ret_w_in, l3_ret_out_gain, l3_ret_w_out,
           l3_ffn_norm, l3_peer_w_q, l3_peer_keys1, l3_peer_keys2, l3_peer_u, l3_peer_v):
    mixers = (retention_mixer, gated_deltanet_mixer, nsa_mixer)
    attn_norms = (l0_attn_norm, l1_attn_norm, l2_attn_norm, l3_attn_norm)
    mixer_args = (
        (l0_ret_w_in, l0_ret_out_gain, l0_ret_w_out),
        (l1_gdn_w_in, l1_gdn_conv_w, l1_gdn_a_log, l1_gdn_dt_bias, l1_gdn_out_gain, l1_gdn_w_out),
        (l2_nsa_w_in, l2_nsa_q_gain, l2_nsa_kc_gain, l2_nsa_ks_gain, l2_nsa_kw_gain,
         l2_nsa_pe_k, l2_nsa_w1_k, l2_nsa_w2_k, l2_nsa_pe_v, l2_nsa_w1_v, l2_nsa_w2_v, l2_nsa_w_out),
        (l3_ret_w_in, l3_ret_out_gain, l3_ret_w_out),
    )
    ffn_norms = (l0_ffn_norm, l1_ffn_norm, l2_ffn_norm, l3_ffn_norm)
    peer_args = (
        (l0_peer_w_q, l0_peer_keys1, l0_peer_keys2, l0_peer_u, l0_peer_v),
        (l1_peer_w_q, l1_peer_keys1, l1_peer_keys2, l1_peer_u, l1_peer_v),
        (l2_peer_w_q, l2_peer_keys1, l2_peer_keys2, l2_peer_u, l2_peer_v),
        (l3_peer_w_q, l3_peer_keys1, l3_peer_keys2, l3_peer_u, l3_peer_v),
    )
    B = x.shape[0]
    D = x.shape[-1]
    vws = [_pack_rows(p[4]) for p in peer_args]

    def mix_and_route(xc, i):
        xc = xc + mixers[i % MIXER_CYCLE](rms_norm(xc, attn_norms[i]), *mixer_args[i])
        w_q, keys1, keys2, _, _ = peer_args[i]
        xn, eid, gate = _peer_route(xc.reshape(-1, D), ffn_norms[i], w_q, keys1, keys2)
        return xc, (xn, eid.reshape(-1, SC_ROWS), gate.reshape(-1, SC_ROWS))

    def stage_all(xs, i):
        out = []
        for xc in xs:
            if out:
                px, (pxn, peid, pgate) = out[-1]
                pxn, xc = lax.optimization_barrier((pxn, xc))
                out[-1] = (px, (pxn, peid, pgate))
            out.append(mix_and_route(xc, i))
        return out

    chains = [x[c * (B // BATCH_CHAINS):(c + 1) * (B // BATCH_CHAINS)] for c in range(BATCH_CHAINS)]
    staged = stage_all(chains, 0)
    for i in range(DEPTH):
        u = peer_args[i][3]
        hs = [_sc_gather_dot(u, eid, xn) for _, (xn, eid, _) in staged]
        xs = []
        for (xc, (xn, eid, gate)), h in zip(staged, hs):
            w = gate * jax.nn.gelu(h, approximate=False)
            xs.append(xc + _sc_gather_wsum(vws[i], eid, w, xn.shape[0]).reshape(xc.shape))
        if i + 1 == DEPTH:
            return jnp.concatenate(xs, axis=0)
        staged = stage_all(xs, i + 1)
```

```python
import functools
import math

import jax
import jax.numpy as jnp
from jax import lax
from jax.experimental import pallas as pl
from jax.experimental.pallas import tpu as pltpu
from jax.experimental.pallas import tpu_sc as plsc

D_MODEL = 1024
DEPTH = 4
f32 = jnp.float32
bf16 = jnp.bfloat16
NORM_EPS = 1e-6
MIXER_CYCLE = 3

RET_HEADS = 8
RET_DK = D_MODEL // RET_HEADS
RET_DV = D_MODEL // RET_HEADS
RET_CHUNK = 128
RET_THETA = 10000.0

GDN_HEADS = 8
GDN_DK = D_MODEL // GDN_HEADS
GDN_DV = D_MODEL // GDN_HEADS
GDN_CONV = 4
GDN_CHUNK = 64

NSA_HEADS = 16
NSA_GROUPS = 4
NSA_HPG = NSA_HEADS // NSA_GROUPS
NSA_DH = D_MODEL // NSA_HEADS
NSA_CMP_BLOCK = 32
NSA_CMP_STRIDE = 16
NSA_CMP_HIDDEN = 256
NSA_SEL_BLOCK = 64
NSA_TOPN = 16
NSA_WINDOW = 512
NSA_FORCE = 1e4
ROPE_THETA = 500000.0
ROPE_DIMS = NSA_DH // 4

PEER_NKEYS = 128
PEER_EXPERTS = PEER_NKEYS * PEER_NKEYS
PEER_HEADS = 8
PEER_QDIM = 256
PEER_TOPK = 16

V7X_VMEM_BYTES = 64 * 1024 * 1024
VMEM_LIMIT = V7X_VMEM_BYTES * 3 // 4
VMEM_LIMIT_NSA = V7X_VMEM_BYTES * 7 // 8
MM_TM, MM_TN = 512, 1024


def _mm_kernel(a_ref, b_ref, o_ref):
    o_ref[...] = jnp.dot(a_ref[...].astype(bf16), b_ref[...].astype(bf16),
                         preferred_element_type=f32)


def _mm(a, b):
    M, K = a.shape
    _, N = b.shape
    tm = MM_TM
    tn = MM_TN if N % MM_TN == 0 else N
    return pl.pallas_call(
        _mm_kernel,
        grid=(M // tm, N // tn),
        in_specs=[pl.BlockSpec((tm, K), lambda i, j: (i, 0)),
                  pl.BlockSpec((K, tn), lambda i, j: (0, j))],
        out_specs=pl.BlockSpec((tm, tn), lambda i, j: (i, j)),
        out_shape=jax.ShapeDtypeStruct((M, N), f32),
        compiler_params=pltpu.CompilerParams(
            dimension_semantics=("parallel", "parallel"),
            vmem_limit_bytes=VMEM_LIMIT),
        name="proj_matmul",
    )(a, b)


def _proj(h, w):
    B, T, D = h.shape
    return _mm(h.reshape(B * T, D), w).reshape(B, T, -1)


def rms_norm(x, gain):
    xf = x.astype(f32)
    y = xf * lax.rsqrt(jnp.mean(xf * xf, axis=-1, keepdims=True) + NORM_EPS)
    return (y * gain.astype(f32)).astype(x.dtype)


def l2_norm(x):
    xf = x.astype(f32)
    return xf * lax.rsqrt(jnp.sum(xf * xf, axis=-1, keepdims=True) + NORM_EPS)


def rotary(x, pos, rot_dims, theta):
    half = rot_dims // 2
    inv_freq = jnp.power(theta, -jnp.arange(half, dtype=f32) / half)
    ang = pos.astype(f32)[:, None] * inv_freq[None, :]
    cos, sin = jnp.cos(ang), jnp.sin(ang)
    xf = x.astype(f32)
    x1, x2 = xf[..., :half], xf[..., half:rot_dims]
    out = jnp.concatenate([x1 * cos - x2 * sin, x2 * cos + x1 * sin, xf[..., rot_dims:]], axis=-1)
    return out.astype(x.dtype)


def causal_depthwise_conv(x, w):
    K, C = w.shape
    return lax.conv_general_dilated(x, w.astype(x.dtype)[:, None, :], window_strides=(1,),
                                    padding=[(K - 1, 0)], dimension_numbers=('NWC', 'WIO', 'NWC'),
                                    feature_group_count=C)


def rotary_bthd(x, pos, rot_dims, theta):
    half = rot_dims // 2
    inv_freq = jnp.power(theta, -jnp.arange(half, dtype=f32) / half)
    ang = pos.astype(f32)[:, None] * inv_freq[None, :]
    cos, sin = jnp.cos(ang)[:, None, :], jnp.sin(ang)[:, None, :]
    xf = x.astype(f32)
    x1, x2 = xf[..., :half], xf[..., half:rot_dims]
    out = jnp.concatenate([x1 * cos - x2 * sin, x2 * cos + x1 * sin, xf[..., rot_dims:]], axis=-1)
    return out.astype(x.dtype)


def _ret_kernel(q_ref, k_ref, v_ref, intra_ref, qd_ref, kd_ref, cd_ref, o_ref, state):
    H, dk, dv = RET_HEADS, RET_DK, RET_DV

    @pl.when(pl.program_id(1) == 0)
    def _():
        state[...] = jnp.zeros_like(state)

    dn_t = (((1,), (1,)), ((), ()))
    dn_k = (((0,), (0,)), ((), ()))
    for h in range(H):
        q = q_ref[:, h * dk:(h + 1) * dk].astype(bf16)
        k = k_ref[:, h * dk:(h + 1) * dk]
        v = v_ref[:, h * dv:(h + 1) * dv].astype(bf16)
        st = state[h]
        s = lax.dot_general(q, k.astype(bf16), dn_t, preferred_element_type=f32) * intra_ref[h]
        o_ref[:, h * dv:(h + 1) * dv] = (
            jnp.dot(s.astype(bf16), v, preferred_element_type=f32)
            + jnp.dot(q, st.astype(bf16), preferred_element_type=f32) * qd_ref[h])
        kk = (k * kd_ref[h]).astype(bf16)
        state[h] = st * cd_ref[h] + lax.dot_general(kk, v, dn_k, preferred_element_type=f32)


def retention_core(q, k, v):
    B, T, _ = q.shape
    H, dk, dv, C = RET_HEADS, RET_DK, RET_DV, RET_CHUNK
    log_gamma = jnp.log1p(-jnp.power(2.0, -5.0 - jnp.arange(H, dtype=f32)))
    idx = jnp.arange(C, dtype=f32)
    rel = idx[:, None] - idx[None, :]
    intra = jnp.where(rel >= 0, jnp.exp(log_gamma[:, None, None] * jnp.maximum(rel, 0.0)), 0.0)
    q_decay = jnp.broadcast_to(jnp.exp(log_gamma[:, None] * (idx + 1.0))[:, :, None], (H, C, dv))
    k_decay = jnp.broadcast_to(jnp.exp(log_gamma[:, None] * (C - 1.0 - idx))[:, :, None], (H, C, dk))
    chunk_decay = jnp.broadcast_to(jnp.exp(log_gamma * C)[:, None, None], (H, 1, dv))

    def blk(w):
        return pl.BlockSpec((None, C, w), lambda b, c: (b, c, 0))

    def const(a):
        return pl.BlockSpec(a.shape, lambda b, c: (0, 0, 0))

    return pl.pallas_call(
        _ret_kernel,
        grid=(B, T // C),
        in_specs=[blk(H * dk), blk(H * dk), blk(H * dv),
                  const(intra), const(q_decay), const(k_decay), const(chunk_decay)],
        out_specs=blk(H * dv),
        out_shape=jax.ShapeDtypeStruct((B, T, H * dv), f32),
        scratch_shapes=[pltpu.VMEM((H, dk, dv), f32)],
        compiler_params=pltpu.CompilerParams(dimension_semantics=("parallel", "arbitrary")),
        name="retention_core",
    )(q, k, v, intra, q_decay, k_decay, chunk_decay)


def retention_mixer(h, w_in, out_gain, w_out):
    B, T, D = h.shape
    H, dk, dv = RET_HEADS, RET_DK, RET_DV
    q, k, v, g = jnp.split(_proj(h, w_in), 4, axis=-1)
    pos = jnp.arange(T)
    q = rotary_bthd(q.reshape(B, T, H, dk), pos, dk, RET_THETA).reshape(B, T, H * dk)
    k = (rotary_bthd(k.reshape(B, T, H, dk), pos, dk, RET_THETA) * (dk ** -0.5)).reshape(B, T, H * dk)
    o = retention_core(q, k, v).reshape(B, T, H, dv)
    o = rms_norm(o, out_gain.reshape(H, dv)).reshape(B, T, H * dv).astype(h.dtype)
    return _proj(jax.nn.silu(g) * o, w_out)


def _gdn_kernel(q_ref, k_ref, v_ref, gc_ref, gr_ref, b_ref, o_ref, state):
    H, dk, dv, C = GDN_HEADS, GDN_DK, GDN_DV, GDN_CHUNK

    @pl.when(pl.program_id(1) == 0)
    def _():
        state[...] = jnp.zeros_like(state)

    dn_t = (((1,), (1,)), ((), ()))
    dn_k = (((0,), (0,)), ((), ()))
    hi = lax.Precision.HIGHEST
    row = lax.broadcasted_iota(jnp.int32, (C, C), 0)
    col = lax.broadcasted_iota(jnp.int32, (C, C), 1)
    eye = (row == col).astype(f32)

    def mm(a, b):
        return jnp.dot(a.astype(bf16), b.astype(bf16), preferred_element_type=f32)

    for h in range(H):
        q = q_ref[:, h * dk:(h + 1) * dk]
        k = k_ref[:, h * dk:(h + 1) * dk]
        v = v_ref[:, h * dv:(h + 1) * dv]
        gc = gc_ref[:, h:h + 1]
        gr = gr_ref[h:h + 1, :]
        beta = b_ref[:, h:h + 1]
        g_last = gr[:, C - 1:C]
        S = state[h]
        decay = jnp.where(row >= col, jnp.exp(jnp.minimum(gc - gr, 0.0)), 0.0)
        k_beta = k * beta
        kk = lax.dot_general(k_beta.astype(bf16), k.astype(bf16), dn_t, preferred_element_type=f32)
        lower = jnp.where(row > col, kk * decay, 0.0)
        P = -lower
        T = eye + P
        n = 1
        while n * 2 < C:
            P = jnp.dot(P, P, precision=hi, preferred_element_type=f32)
            T = T + jnp.dot(T, P, precision=hi, preferred_element_type=f32)
            n *= 2
        u = mm(T, v * beta)
        w = mm(T, k_beta * jnp.exp(gc))
        qk = lax.dot_general(q.astype(bf16), k.astype(bf16), dn_t, preferred_element_type=f32) * decay
        v_new = u - mm(w, S)
        o_ref[:, h * dv:(h + 1) * dv] = mm(q * jnp.exp(gc), S) + mm(qk, v_new)
        k_dec = (k * jnp.exp(g_last - gc)).astype(bf16)
        state[h] = S * jnp.exp(g_last) + lax.dot_general(k_dec, v_new.astype(bf16), dn_k, preferred_element_type=f32)


def gdn_core(q, k, v, g_cum, beta):
    B, T, _ = q.shape
    H, dk, dv, C = GDN_HEADS, GDN_DK, GDN_DV, GDN_CHUNK
    n = T // C
    g_row = g_cum.reshape(B, n, C, H).transpose(0, 1, 3, 2)

    def blk(w):
        return pl.BlockSpec((None, C, w), lambda b, c: (b, c, 0))

    return pl.pallas_call(
        _gdn_kernel,
        grid=(B, n),
        in_specs=[blk(H * dk), blk(H * dk), blk(H * dv), blk(H),
                  pl.BlockSpec((None, None, H, C), lambda b, c: (b, c, 0, 0)), blk(H)],
        out_specs=blk(H * dv),
        out_shape=jax.ShapeDtypeStruct((B, T, H * dv), f32),
        scratch_shapes=[pltpu.VMEM((H, dk, dv), f32)],
        compiler_params=pltpu.CompilerParams(dimension_semantics=("parallel", "arbitrary")),
        name="gdn_core",
    )(q, k, v, g_cum, g_row, beta)


def gated_deltanet_mixer(h, w_in, conv_w, a_log, dt_bias, out_gain, w_out):
    B, T, D = h.shape
    H, dk, dv, C = GDN_HEADS, GDN_DK, GDN_DV, GDN_CHUNK
    n_qk, n_v = H * dk, H * dv
    n_conv = 2 * n_qk + n_v
    n = T // C
    proj = jnp.dot(h, w_in)
    qkv = jax.nn.silu(causal_depthwise_conv(proj[..., :n_conv], conv_w))
    z = proj[..., n_conv:n_conv + n_v]
    beta = jax.nn.sigmoid(proj[..., n_conv + n_v:n_conv + n_v + H].astype(f32))
    a = proj[..., n_conv + n_v + H:].astype(f32)
    g = -jnp.exp(a_log.astype(f32)) * jax.nn.softplus(a + dt_bias.astype(f32))

    g_cum = jnp.cumsum(g.reshape(B, n, C, H), axis=2).reshape(B, T, H)
    q = (l2_norm(qkv[..., :n_qk].reshape(B, T, H, dk)) * (dk ** -0.5)).reshape(B, T, n_qk)
    k = l2_norm(qkv[..., n_qk:2 * n_qk].reshape(B, T, H, dk)).reshape(B, T, n_qk)
    v = qkv[..., 2 * n_qk:].astype(f32)
    o = gdn_core(q, k, v, g_cum, beta).reshape(B, T, H, dv)
    o = rms_norm(o, out_gain) * jax.nn.silu(z.reshape(B, T, H, dv).astype(f32))
    return _proj(o.reshape(B, T, H * dv).astype(h.dtype), w_out)


def compress_blocks(zz, pe, w1, w2):
    Bz, Gz, Tz, dz = zz.shape
    assert NSA_CMP_BLOCK == 2 * NSA_CMP_STRIDE and Tz % NSA_CMP_STRIDE == 0
    pieces = zz.reshape(Bz, Gz, Tz // NSA_CMP_STRIDE, NSA_CMP_STRIDE, dz)
    blocks = jnp.concatenate([pieces[:, :, :-1], pieces[:, :, 1:]], axis=3) + pe
    flat = blocks.reshape(blocks.shape[0], blocks.shape[1], blocks.shape[2], -1)
    return jax.nn.gelu(flat @ w1, approximate=False) @ w2


NSA_TQ = 128
NSA_TK = 512
LANE = 128
NEG = -1e30
NSA_GATE_COLS = 16


def _nsa_kernel(q_ref, kc_ref, vc_ref, ks_ref, vs_ref, kw_ref, vw_ref, g_ref, ovl_ref, exp_ref, o_ref, *, top_n):
    Hg, TQ, dh = q_ref.shape
    TK, W = NSA_TK, NSA_WINDOW
    ncp = kc_ref.shape[0]
    nsp = exp_ref.shape[0]
    t0 = pl.program_id(2) * TQ
    q2 = q_ref[...].reshape(Hg * TQ, dh)
    tpos = t0 + lax.broadcasted_iota(jnp.int32, (TQ, 1), 0)
    dn = (((1,), (1,)), ((), ()))

    def softmax_rows(s, mask):
        s3 = jnp.where(mask[None], s.reshape(Hg, TQ, -1), NEG)
        m = jnp.max(s3, axis=-1, keepdims=True)
        e = jnp.where(mask[None], jnp.exp(s3 - m), 0.0)
        return e / jnp.maximum(jnp.sum(e, axis=-1, keepdims=True), 1e-30)

    s = lax.dot_general(q2, kc_ref[...], dn, preferred_element_type=f32)
    cend = lax.broadcasted_iota(jnp.int32, (1, ncp), 1) * NSA_CMP_STRIDE + (NSA_CMP_BLOCK - 1)
    p = softmax_rows(s, cend <= tpos)
    o_cmp = jnp.dot(p.reshape(Hg * TQ, ncp).astype(bf16), vc_ref[...], preferred_element_type=f32)

    p_all = jnp.concatenate([p[h] for h in range(Hg)], axis=-1).astype(bf16)
    imp = jnp.dot(p_all, ovl_ref[...], preferred_element_type=f32)
    blk = lax.broadcasted_iota(jnp.int32, (1, nsp), 1)
    cur = tpos // NSA_SEL_BLOCK
    forced = (blk == 0) | (blk == cur) | (blk == cur - 1)
    score = jnp.where(forced, NSA_FORCE, jnp.where(blk <= cur, imp, -NSA_FORCE))
    n_real = exp_ref.shape[1] // NSA_SEL_BLOCK
    score = jnp.where(blk < n_real, score, -jnp.inf)
    st = score.T
    rows = lax.broadcasted_iota(jnp.int32, st.shape, 0)
    sel_t = jnp.zeros(st.shape, f32)
    for _ in range(top_n):
        m = jnp.max(st, axis=0, keepdims=True)
        first = jnp.min(jnp.where(st == m, rows, nsp), axis=0, keepdims=True)
        hit = rows == first
        sel_t = jnp.where(hit, 1.0, sel_t)
        st = jnp.where(hit, -jnp.inf, st)
    sel = sel_t.T.astype(bf16)

    n_tiles = (t0 + TQ + TK - 1) // TK

    def sel_tile(kt, carry):
        m, l, acc = carry
        koff = pl.multiple_of(kt * TK, TK)
        k_t = ks_ref[pl.ds(koff, TK), :]
        v_t = vs_ref[pl.ds(koff, TK), :]
        s = lax.dot_general(q2, k_t, dn, preferred_element_type=f32)
        picked = jnp.dot(sel, exp_ref[:, pl.ds(koff, TK)], preferred_element_type=f32)
        kpos = koff + lax.broadcasted_iota(jnp.int32, (1, TK), 1)
        mask = ((picked > 0.5) & (kpos <= tpos))[None]
        s3 = jnp.where(mask, s.reshape(Hg, TQ, TK), NEG)
        m_new = jnp.maximum(m, jnp.max(s3, axis=-1, keepdims=True))
        alpha = jnp.exp(m - m_new)
        e = jnp.where(mask, jnp.exp(s3 - m_new), 0.0)
        l = l * alpha + jnp.sum(e, axis=-1, keepdims=True)
        pv = jnp.dot(e.reshape(Hg * TQ, TK).astype(bf16), v_t, preferred_element_type=f32)
        return m_new, l, acc * alpha + pv.reshape(Hg, TQ, dh)

    m0 = jnp.full((Hg, TQ, 1), NEG, f32)
    l0 = jnp.zeros((Hg, TQ, 1), f32)
    a0 = jnp.zeros((Hg, TQ, dh), f32)
    _, l, acc = lax.fori_loop(0, n_tiles, sel_tile, (m0, l0, a0))
    o_slc = acc / jnp.maximum(l, 1e-30)

    woff = pl.multiple_of(t0, TQ)
    k_w = kw_ref[pl.ds(woff, W + TQ), :]
    v_w = vw_ref[pl.ds(woff, W + TQ), :]
    s = lax.dot_general(q2, k_w, dn, preferred_element_type=f32)
    wpos = t0 - W + lax.broadcasted_iota(jnp.int32, (1, W + TQ), 1)
    dpos = tpos - wpos
    p = softmax_rows(s, (dpos >= 0) & (dpos < W) & (wpos >= 0))
    o_win = jnp.dot(p.reshape(Hg * TQ, W + TQ).astype(bf16), v_w, preferred_element_type=f32)

    g = g_ref[...]
    o_cmp = o_cmp.reshape(Hg, TQ, dh)
    o_win = o_win.reshape(Hg, TQ, dh)
    for h in range(Hg):
        o_ref[:, h * dh:(h + 1) * dh] = (g[:, h:h + 1] * o_cmp[h] + g[:, Hg + h:Hg + h + 1] * o_slc[h]
                                         + g[:, 2 * Hg + h:2 * Hg + h + 1] * o_win[h])


def nsa_attention(q, k_cmp, v_cmp, ks, vs, kw, vw, gates):
    B, G, Hg, T, dh = q.shape
    L, S_, SEL, W, TQ = NSA_CMP_BLOCK, NSA_CMP_STRIDE, NSA_SEL_BLOCK, NSA_WINDOW, NSA_TQ
    n_cmp = k_cmp.shape[2]
    n_sel = T // SEL
    top_n = min(NSA_TOPN, n_sel)
    ncp = -(-n_cmp // LANE) * LANE
    nsp = -(-n_sel // LANE) * LANE
    cmp_start = jnp.arange(n_cmp) * S_
    sel_start = jnp.arange(n_sel) * SEL
    overlap = jnp.clip(jnp.minimum(cmp_start[:, None] + L, sel_start[None, :] + SEL)
                       - jnp.maximum(cmp_start[:, None], sel_start[None, :]), 0).astype(f32) / L
    ovl = jnp.pad(overlap, ((0, ncp - n_cmp), (0, nsp - n_sel)))
    ovl = jnp.tile(ovl, (Hg, 1)).astype(bf16)
    expand = (jnp.arange(nsp)[:, None] == (jnp.arange(T)[None, :] // SEL)).astype(bf16)
    padc = ((0, 0), (0, 0), (0, ncp - n_cmp), (0, 0))
    padw = ((0, 0), (0, 0), (W, 0), (0, 0))
    g2 = gates.transpose(0, 2, 4, 1, 3).reshape(B, G, T, 3 * Hg)
    g2 = jnp.pad(g2, ((0, 0), (0, 0), (0, 0), (0, NSA_GATE_COLS - 3 * Hg)))

    def kv(n):
        return pl.BlockSpec((None, None, n, dh), lambda b, g, i: (b, g, 0, 0))

    return pl.pallas_call(
        functools.partial(_nsa_kernel, top_n=top_n),
        grid=(B, G, T // TQ),
        in_specs=[pl.BlockSpec((None, None, Hg, TQ, dh), lambda b, g, i: (b, g, 0, i, 0)),
                  kv(ncp), kv(ncp), kv(T), kv(T), kv(T + W), kv(T + W),
                  pl.BlockSpec((None, None, TQ, NSA_GATE_COLS), lambda b, g, i: (b, g, i, 0)),
                  pl.BlockSpec((Hg * ncp, nsp), lambda b, g, i: (0, 0)),
                  pl.BlockSpec((nsp, T), lambda b, g, i: (0, 0))],
        out_specs=pl.BlockSpec((None, TQ, Hg * dh), lambda b, g, i: (b, i, g)),
        out_shape=jax.ShapeDtypeStruct((B, T, G * Hg * dh), f32),
        compiler_params=pltpu.CompilerParams(
            dimension_semantics=("parallel", "parallel", "arbitrary"), vmem_limit_bytes=VMEM_LIMIT_NSA),
        name="nsa_attention",
    )(q.astype(bf16), jnp.pad(k_cmp, padc).astype(bf16), jnp.pad(v_cmp, padc).astype(bf16),
      ks.astype(bf16), vs.astype(bf16), jnp.pad(kw, padw).astype(bf16), jnp.pad(vw, padw).astype(bf16),
      g2, ovl, expand)


def nsa_mixer(h, w_in, q_gain, kc_gain, ks_gain, kw_gain, pe_k, w1_k, w2_k, pe_v, w1_v, w2_v, w_out):
    B, T, D = h.shape
    H, G, Hg, dh = NSA_HEADS, NSA_GROUPS, NSA_HPG, NSA_DH
    kvw = G * dh
    L, S_ = NSA_CMP_BLOCK, NSA_CMP_STRIDE
    n_cmp = (T - L) // S_ + 1

    proj = _proj(h, w_in)
    q = proj[..., :H * dh].reshape(B, T, H, dh)
    kc, vc, ks, vs, kw, vw = [proj[..., H * dh + j * kvw:H * dh + (j + 1) * kvw]
                              .reshape(B, T, G, dh).transpose(0, 2, 1, 3) for j in range(6)]
    gates = jax.nn.sigmoid(proj[..., H * dh + 6 * kvw:].astype(f32))
    gates = gates.reshape(B, T, 3, H).transpose(0, 2, 3, 1).reshape(B, 3, G, Hg, T)

    pos = jnp.arange(T)
    q = rotary(rms_norm(q, q_gain).transpose(0, 2, 1, 3), pos, ROPE_DIMS, ROPE_THETA)
    q = q.reshape(B, G, Hg, T, dh) * (dh ** -0.5)
    ks = rotary(rms_norm(ks, ks_gain), pos, ROPE_DIMS, ROPE_THETA)
    kw = rotary(rms_norm(kw, kw_gain), pos, ROPE_DIMS, ROPE_THETA)

    cmp_start = jnp.arange(n_cmp) * S_
    cmp_end = cmp_start + L - 1
    k_cmp = rotary(rms_norm(compress_blocks(kc, pe_k, w1_k, w2_k), kc_gain), cmp_end,
                   ROPE_DIMS, ROPE_THETA)
    v_cmp = compress_blocks(vc, pe_v, w1_v, w2_v)

    o = nsa_attention(q, k_cmp, v_cmp, ks, vs, kw, vw, gates)
    return _proj(o, w_out)


PEER_TM = 128
SC_LANES = 16
SC_WORKERS = 32
SC_ROWS = 32
SC_TOK_BLOCK = 8
PEER_PICKS = PEER_HEADS * PEER_TOPK
BATCH_CHAINS = 2


def _top_rows(s, k, order=None, payload=None):
    if order is None:
        order = lax.broadcasted_iota(jnp.int32, s.shape, 0)
    big = jnp.int32(2 ** 30)
    vals, ids = [], []
    for _ in range(k):
        m = jnp.max(s, axis=0, keepdims=True)
        first = jnp.min(jnp.where(s == m, order, big), axis=0, keepdims=True)
        hit = order == first
        vals.append(m)
        if payload is None:
            ids.append(first)
        else:
            ids.append(jnp.max(jnp.where(hit, payload, -1), axis=0, keepdims=True))
        s = jnp.where(hit, -jnp.inf, s)
    return jnp.concatenate(vals, axis=0), jnp.concatenate(ids, axis=0)


def _pair_candidates(v1, i1, v2, i2):
    K, S = PEER_TOPK, 8
    rows8 = lax.broadcasted_iota(jnp.int32, (S, v1.shape[1]), 0)
    vals = [v1[i:i + 1, :] + v2[0:S, :] for i in range(S)]
    flat = [i * K + rows8 for i in range(S)]
    cid = [i1[i:i + 1, :] * PEER_NKEYS + i2[0:S, :] for i in range(S)]
    vals.append(v1[0:1, :] + v2[S:K, :])
    flat.append(S + rows8)
    cid.append(i1[0:1, :] * PEER_NKEYS + i2[S:K, :])
    vals.append(v1[S:K, :] + v2[0:1, :])
    flat.append((S + rows8) * K)
    cid.append(i1[S:K, :] * PEER_NKEYS + i2[0:1, :])
    return tuple(jnp.concatenate(xs, axis=0) for xs in (vals, flat, cid))


def _peer_route_kernel(x_ref, g_ref, wq_ref, k1_ref, k2_ref, xn_ref, eid_ref, gate_ref, eid_t, gate_t):
    x = x_ref[...]
    xn = x * lax.rsqrt(jnp.mean(x * x, axis=-1, keepdims=True) + NORM_EPS) * g_ref[...]
    xn_ref[...] = xn
    q = jnp.dot(xn.astype(bf16), wq_ref[...], preferred_element_type=f32)
    half = PEER_QDIM // 2
    K = PEER_TOPK
    k1 = k1_ref[...]
    k2 = k2_ref[...]
    dn = (((1,), (1,)), ((), ()))
    for h in range(PEER_HEADS):
        q1 = q[:, h * PEER_QDIM:h * PEER_QDIM + half].astype(bf16)
        q2 = q[:, h * PEER_QDIM + half:(h + 1) * PEER_QDIM].astype(bf16)
        s1 = lax.dot_general(k1, q1, dn, preferred_element_type=f32)
        s2 = lax.dot_general(k2, q2, dn, preferred_element_type=f32)
        v1, i1 = _top_rows(s1, K)
        v2, i2 = _top_rows(s2, K)
        cand, flat, cid = _pair_candidates(v1, i1, v2, i2)
        sc, eid = _top_rows(cand, K, order=flat, payload=cid)
        e = jnp.exp(sc - jnp.max(sc, axis=0, keepdims=True))
        eid_t[h * K:(h + 1) * K, :] = eid
        gate_t[h * K:(h + 1) * K, :] = e / jnp.sum(e, axis=0, keepdims=True)
    eid_ref[...] = eid_t[...].T
    gate_ref[...] = gate_t[...].T


def _peer_route(x2, gain, w_q, keys1, keys2):
    N, D = x2.shape
    nq = PEER_HEADS * PEER_QDIM
    P = PEER_PICKS
    tm = PEER_TM
    return pl.pallas_call(
        _peer_route_kernel,
        grid=(N // tm,),
        in_specs=[pl.BlockSpec((tm, D), lambda i: (i, 0)),
                  pl.BlockSpec((1, D), lambda i: (0, 0)),
                  pl.BlockSpec((D, nq), lambda i: (0, 0)),
                  pl.BlockSpec(keys1.shape, lambda i: (0, 0)),
                  pl.BlockSpec(keys2.shape, lambda i: (0, 0))],
        out_specs=[pl.BlockSpec((tm, D), lambda i: (i, 0)),
                   pl.BlockSpec((tm, P), lambda i: (i, 0)),
                   pl.BlockSpec((tm, P), lambda i: (i, 0))],
        out_shape=[jax.ShapeDtypeStruct((N, D), f32),
                   jax.ShapeDtypeStruct((N, P), jnp.int32),
                   jax.ShapeDtypeStruct((N, P), f32)],
        scratch_shapes=[pltpu.VMEM((P, tm), jnp.int32), pltpu.VMEM((P, tm), f32)],
        compiler_params=pltpu.CompilerParams(
            dimension_semantics=("parallel",), vmem_limit_bytes=VMEM_LIMIT),
        name="peer_route",
    )(x2, gain.reshape(1, D), w_q.astype(bf16), keys1.astype(bf16), keys2.astype(bf16))


_SC_PARAMS = pltpu.CompilerParams(needs_layout_passes=False)


def _sc_mesh():
    return plsc.VectorSubcoreMesh(core_axis_name="c", subcore_axis_name="s")


def _sc_gather_dot(u, idx, x2):
    N, D = x2.shape
    G = PEER_PICKS // SC_ROWS
    R, L = SC_ROWS, SC_LANES
    tpw = N // SC_WORKERS
    nblk = tpw // SC_TOK_BLOCK
    items = SC_TOK_BLOCK * G
    nch = D // L

    @functools.partial(
        pl.kernel, mesh=_sc_mesh(), compiler_params=_SC_PARAMS,
        out_type=jax.ShapeDtypeStruct((N * G, R), f32),
        scratch_types=[
            pltpu.VMEM((items, R), jnp.int32),
            pltpu.VMEM((SC_TOK_BLOCK, D), f32),
            pltpu.VMEM((items, R), f32),
            pltpu.VMEM((R, D), f32),
            pltpu.VMEM((R, D), f32),
            pltpu.SemaphoreType.DMA,
            pltpu.SemaphoreType.DMA,
        ],
        name="peer_sc_dot",
    )
    def k(u_hbm, idx_hbm, x_hbm, h_hbm, idx_v, x_v, h_v, rows0, rows1, sem0, sem1):
        wid = lax.axis_index("s") * 2 + lax.axis_index("c")
        rows = (rows0, rows1)
        sems = (sem0, sem1)
        lane = lax.iota(jnp.int32, L)

        def gather(item, b):
            return pltpu.make_async_copy(u_hbm.at[idx_v.at[item]], rows[b], sems[b])

        def compute(item, b):
            tl = item // G
            rv = rows[b]

            def chunk(c, accs):
                off = pl.multiple_of(c * L, L)
                xc = x_v[tl, pl.ds(off, L)]
                return tuple(accs[r] + rv[r, pl.ds(off, L)] * xc for r in range(R))

            accs = lax.fori_loop(0, nch, chunk, tuple(jnp.zeros((L,), f32) for _ in range(R)))
            for half in range(R // L):
                out = jnp.zeros((L,), f32)
                for r in range(L):
                    out = jnp.where(lane == r, jnp.sum(accs[half * L + r]), out)
                h_v[item, pl.ds(half * L, L)] = out

        @pl.loop(0, nblk)
        def _(blk):
            tok0 = wid * tpw + blk * SC_TOK_BLOCK
            it0 = tok0 * G
            pltpu.sync_copy(idx_hbm.at[pl.ds(it0, items)], idx_v)
            pltpu.sync_copy(x_hbm.at[pl.ds(tok0, SC_TOK_BLOCK)], x_v)
            gather(0, 0).start()

            @pl.loop(0, items, step=2)
            def _(it):
                gather(it + 1, 1).start()
                gather(it, 0).wait()
                compute(it, 0)

                @pl.when(it + 2 < items)
                def _():
                    gather(it + 2, 0).start()

                gather(it + 1, 1).wait()
                compute(it + 1, 1)

            pltpu.sync_copy(h_v, h_hbm.at[pl.ds(it0, items)])

    return k(u, idx, x2)


def _pack_rows(t):
    D = t.shape[1]
    b = lax.bitcast_convert_type(t.astype(bf16), jnp.uint16).astype(jnp.uint32)
    return lax.bitcast_convert_type(b[:, :D // 2] | (b[:, D // 2:] << 16), jnp.int32)


def _unpack_words(w):
    return plsc.bitcast(lax.shift_left(w, 16), f32), plsc.bitcast(w & jnp.int32(-65536), f32)


def _sc_gather_wsum(vw, idx, w, N):
    DW = vw.shape[1]
    D = 2 * DW
    G = PEER_PICKS // SC_ROWS
    R, L = SC_ROWS, SC_LANES
    tpw = N // SC_WORKERS
    nblk = tpw // SC_TOK_BLOCK
    items = SC_TOK_BLOCK * G
    nch = DW // L

    @functools.partial(
        pl.kernel, mesh=_sc_mesh(), compiler_params=_SC_PARAMS,
        out_type=jax.ShapeDtypeStruct((N, D), f32),
        scratch_types=[
            pltpu.VMEM((items, R), jnp.int32),
            pltpu.VMEM((items, R), f32),
            pltpu.VMEM((SC_TOK_BLOCK, D), f32),
            pltpu.VMEM((R, DW), jnp.int32),
            pltpu.VMEM((R, DW), jnp.int32),
            pltpu.SemaphoreType.DMA,
            pltpu.SemaphoreType.DMA,
        ],
        name="peer_sc_wsum",
    )
    def k(v_hbm, idx_hbm, w_hbm, o_hbm, idx_v, w_v, o_v, rows0, rows1, sem0, sem1):
        wid = lax.axis_index("s") * 2 + lax.axis_index("c")
        rows = (rows0, rows1)
        sems = (sem0, sem1)
        lane = lax.iota(jnp.int32, L)

        def gather(item, b):
            return pltpu.make_async_copy(v_hbm.at[idx_v.at[item]], rows[b], sems[b])

        def tree_sum(xs):
            while len(xs) > 1:
                xs = [a + b for a, b in zip(xs[::2], xs[1::2])]
            return xs[0]

        def compute(item, b):
            tl = item // G
            first = (item % G) == 0
            rv = rows[b]
            for half in range(R // L):
                wv = w_v[item, pl.ds(half * L, L)]
                ws = [jnp.sum(jnp.where(lane == r, wv, 0.0)) for r in range(L)]

                @plsc.parallel_loop(0, nch, unroll=2)
                def _(c):
                    off = pl.multiple_of(c * L, L)
                    los, his = [], []
                    for r in range(L):
                        lo, hi = _unpack_words(rv[half * L + r, pl.ds(off, L)])
                        los.append(lo * ws[r])
                        his.append(hi * ws[r])
                    plo = o_v[tl, pl.ds(off, L)]
                    phi = o_v[tl, pl.ds(DW + off, L)]
                    if half == 0:
                        plo = jnp.where(first, 0.0, plo)
                        phi = jnp.where(first, 0.0, phi)
                    o_v[tl, pl.ds(off, L)] = tree_sum(los) + plo
                    o_v[tl, pl.ds(DW + off, L)] = tree_sum(his) + phi

        @pl.loop(0, nblk)
        def _(blk):
            tok0 = wid * tpw + blk * SC_TOK_BLOCK
            it0 = tok0 * G
            pltpu.sync_copy(idx_hbm.at[pl.ds(it0, items)], idx_v)
            pltpu.sync_copy(w_hbm.at[pl.ds(it0, items)], w_v)
            gather(0, 0).start()

            @pl.loop(0, items, step=2)
            def _(it):
                gather(it + 1, 1).start()
                gather(it, 0).wait()
                compute(it, 0)

                @pl.when(it + 2 < items)
                def _():
                    gather(it + 2, 0).start()

                gather(it + 1, 1).wait()
                compute(it + 1, 1)

            pltpu.sync_copy(o_v, o_hbm.at[pl.ds(tok0, SC_TOK_BLOCK)])

    return k(vw, idx, w)


def kernel(x,
           l0_attn_norm, l0_ret_w_in, l0_ret_out_gain, l0_ret_w_out,
           l0_ffn_norm, l0_peer_w_q, l0_peer_keys1, l0_peer_keys2, l0_peer_u, l0_peer_v,
           l1_attn_norm, l1_gdn_w_in, l1_gdn_conv_w, l1_gdn_a_log, l1_gdn_dt_bias, l1_gdn_out_gain, l1_gdn_w_out,
           l1_ffn_norm, l1_peer_w_q, l1_peer_keys1, l1_peer_keys2, l1_peer_u, l1_peer_v,
           l2_attn_norm, l2_nsa_w_in, l2_nsa_q_gain, l2_nsa_kc_gain, l2_nsa_ks_gain, l2_nsa_kw_gain,
           l2_nsa_pe_k, l2_nsa_w1_k, l2_nsa_w2_k, l2_nsa_pe_v, l2_nsa_w1_v, l2_nsa_w2_v, l2_nsa_w_out,
           l2_ffn_norm, l2_peer_w_q, l2_peer_keys1, l2_peer_keys2, l2_peer_u, l2_peer_v,
           l3_attn_norm, l3_ret_w_in, l3_ret_out_gain, l3_ret_w_out,
           l3_ffn_norm, l3_peer_w_q, l3_peer_keys1, l3_peer_keys2, l3_peer_u, l3_peer_v):
    mixers = (retention_mixer, gated_deltanet_mixer, nsa_mixer)
    attn_norms = (l0_attn_norm, l1_attn_norm, l2_attn_norm, l3_attn_norm)
    mixer_args = (
        (l0_ret_w_in, l0_ret_out_gain, l0_ret_w_out),
        (l1_gdn_w_in, l1_gdn_conv_w, l1_gdn_a_log, l1_gdn_dt_bias, l1_gdn_out_gain, l1_gdn_w_out),
        (l2_nsa_w_in, l2_nsa_q_gain, l2_nsa_kc_gain, l2_nsa_ks_gain, l2_nsa_kw_gain,
         l2_nsa_pe_k, l2_nsa_w1_k, l2_nsa_w2_k, l2_nsa_pe_v, l2_nsa_w1_v, l2_nsa_w2_v, l2_nsa_w_out),
        (l3_ret_w_in, l3_ret_out_gain, l3_ret_w_out),
    )
    ffn_norms = (l0_ffn_norm, l1_ffn_norm, l2_ffn_norm, l3_ffn_norm)
    peer_args = (
        (l0_peer_w_q, l0_peer_keys1, l0_peer_keys2, l0_peer_u, l0_peer_v),
        (l1_peer_w_q, l1_peer_keys1, l1_peer_keys2, l1_peer_u, l1_peer_v),
        (l2_peer_w_q, l2_peer_keys1, l2_peer_keys2, l2_peer_u, l2_peer_v),
        (l3_peer_w_q, l3_peer_keys1, l3_peer_keys2, l3_peer_u, l3_peer_v),
    )
    B = x.shape[0]
    D = x.shape[-1]
    vws = [_pack_rows(p[4]) for p in peer_args]

    def mix_and_route(xc, i):
        xc = xc + mixers[i % MIXER_CYCLE](rms_norm(xc, attn_norms[i]), *mixer_args[i])
        w_q, keys1, keys2, _, _ = peer_args[i]
        xn, eid, gate = _peer_route(xc.reshape(-1, D), ffn_norms[i], w_q, keys1, keys2)
        return xc, (xn, eid.reshape(-1, SC_ROWS), gate.reshape(-1, SC_ROWS))

    def stage_all(xs, i):
        out = []
        for xc in xs:
            if out:
                px, (pxn, peid, pgate) = out[-1]
                pxn, xc = lax.optimization_barrier((pxn, xc))
                out[-1] = (px, (pxn, peid, pgate))
            out.append(mix_and_route(xc, i))
        return out

    chains = [x[c * (B // BATCH_CHAINS):(c + 1) * (B // BATCH_CHAINS)] for c in range(BATCH_CHAINS)]
    staged = stage_all(chains, 0)
    for i in range(DEPTH):
        u = peer_args[i][3]
        hs = [_sc_gather_dot(u, eid, xn) for _, (xn, eid, _) in staged]
        xs = []
        for (xc, (xn, eid, gate)), h in zip(staged, hs):
            w = gate * jax.nn.gelu(h, approximate=False)
            xs.append(xc + _sc_gather_wsum(vws[i], eid, w, xn.shape[0]).reshape(xc.shape))
        if i + 1 == DEPTH:
            return jnp.concatenate(xs, axis=0)
        staged = stage_all(xs, i + 1)
```
